```python
import math
import jax
import jax.numpy as jnp
from jax import lax
import numpy as np

D_MODEL = 1024
BATCH = 8
SEQ = 8192
DEPTH = 2

CTX_LEN = 256
GRID_W = 64
HEAD_DIM = 64
GQA_HEADS = 8
GQA_KV_HEADS = 2
GQA_GROUP = GQA_HEADS // GQA_KV_HEADS
DIFF_HEADS = 4
DIFF_DIM = HEAD_DIM
DIFF_VDIM = 2 * DIFF_DIM
GDN_HEADS = 4
GDN_DK = 128
GDN_DV = 128
GDN_CONV = 3
GDN_CHUNK = 64
N_BRANCH = 3
BRANCH_W = 512
N_EXPERTS = 16
EXPERT_FF = 1024
CAPACITY_FACTOR = 2
Q_BLOCK = 128
ROPE_THETA = 10000.0
EPS = 1e-6

A_Q = GQA_HEADS * HEAD_DIM
A_KV = GQA_KV_HEADS * HEAD_DIM
B_QK = DIFF_HEADS * 2 * DIFF_DIM
B_V = DIFF_HEADS * DIFF_VDIM
C_QK = GDN_HEADS * GDN_DK
C_V = GDN_HEADS * GDN_DV
C_GATE = 2 * GDN_HEADS
IN_WIDTHS = (A_Q, A_KV, A_KV, B_QK, B_QK, B_V, C_QK, C_QK, C_V, C_V, C_GATE, C_GATE)
IN_COLS = sum(IN_WIDTHS)
GDN_CONV_CH = 2 * C_QK + C_V

kernel_name = 'hybrid_dit_gqa_diffattn_gdn_ecmoe'


def rms_norm(x, w):
    xf = x.astype(jnp.float32)
    y = xf * lax.rsqrt(jnp.mean(xf * xf, axis=-1, keepdims=True) + EPS)
    return (y * w.astype(jnp.float32)).astype(x.dtype)


def l2_norm(x):
    xf = x.astype(jnp.float32)
    return xf * lax.rsqrt(jnp.sum(xf * xf, axis=-1, keepdims=True) + EPS)


def lambda_init(layer_idx):
    return 0.8 - 0.6 * math.exp(-0.3 * layer_idx)


def axial_rope(n, dim):
    rows = n // GRID_W
    row = jnp.repeat(jnp.arange(rows, dtype=jnp.float32), GRID_W)
    col = jnp.tile(jnp.arange(GRID_W, dtype=jnp.float32), rows)
    axis_dim = dim // 2
    inv_freq = ROPE_THETA ** (-jnp.arange(0, axis_dim, 2, dtype=jnp.float32) / axis_dim)
    ang = jnp.concatenate([row[:, None] * inv_freq, col[:, None] * inv_freq], axis=-1)
    return jnp.cos(ang), jnp.sin(ang)


def apply_rope(x, cos, sin):
    b, n, h, dim = x.shape
    xa = x.reshape(b, n, h, 2, 2, dim // 4)
    x1, x2 = xa[..., 0, :], xa[..., 1, :]
    c = cos.reshape(n, 1, 2, dim // 4).astype(x.dtype)
    s = sin.reshape(n, 1, 2, dim // 4).astype(x.dtype)
    out = jnp.stack([x1 * c - x2 * s, x2 * c + x1 * s], axis=-2)
    return out.reshape(b, n, h, dim)


def qk_heads(t, n_heads, dim, norm_w, rope):
    b, n = t.shape[:2]
    t = rms_norm(t.reshape(b, n, n_heads, dim), norm_w)
    if rope is not None:
        t = apply_rope(t, *rope)
    return t


def _blocks(q):
    b, n = q.shape[:2]
    q = q.reshape((b, n // Q_BLOCK, Q_BLOCK) + q.shape[2:])
    return jnp.moveaxis(q, 1, 0)


def _unblocks(o):
    o = jnp.moveaxis(o, 0, 1)
    return o.reshape((o.shape[0], o.shape[1] * o.shape[2]) + o.shape[3:])


def gqa_attend(q, k, v):
    scale = HEAD_DIM ** -0.5

    def one(qb):
        s = jnp.einsum('bqhgd,bkhd->bhgqk', qb, k, preferred_element_type=jnp.float32) * scale
        p = jax.nn.softmax(s, axis=-1).astype(v.dtype)
        return jnp.einsum('bhgqk,bkhd->bqhgd', p, v)

    return _unblocks(lax.map(one, _blocks(q)))


def diff_attend(q, k, v, lam):
    scale = DIFF_DIM ** -0.5

    def one(qb):
        s = jnp.einsum('bqhcd,bkhcd->bhcqk', qb, k, preferred_element_type=jnp.float32) * scale
        p = jax.nn.softmax(s, axis=-1)
        w = (p[:, :, 0] - lam * p[:, :, 1]).astype(v.dtype)
        return jnp.einsum('bhqk,bkhe->bqhe', w, v)

    return _unblocks(lax.map(one, _blocks(q)))


def gqa_branch(lat, ctx, qn, kn, rope, need_ctx):
    q, k, v = lat
    qc, kc, vc = ctx
    b, n = q.shape[:2]

    def grp(t):
        return t.reshape(t.shape[0], t.shape[1], GQA_KV_HEADS, GQA_GROUP, HEAD_DIM)

    def vh(t):
        return t.reshape(t.shape[0], t.shape[1], GQA_KV_HEADS, HEAD_DIM)

    k_ctx = qk_heads(kc, GQA_KV_HEADS, HEAD_DIM, kn, None)
    v_ctx = vh(vc)
    k_all = jnp.concatenate([qk_heads(k, GQA_KV_HEADS, HEAD_DIM, kn, rope), k_ctx], axis=1)
    v_all = jnp.concatenate([vh(v), v_ctx], axis=1)
    y = gqa_attend(grp(qk_heads(q, GQA_HEADS, HEAD_DIM, qn, rope)), k_all, v_all).reshape(b, n, A_Q)
    yc = None
    if need_ctx:
        yc = gqa_attend(grp(qk_heads(qc, GQA_HEADS, HEAD_DIM, qn, None)), k_ctx, v_ctx)
        yc = yc.reshape(qc.shape[0], qc.shape[1], A_Q)
    return y, yc


def diff_branch(lat, ctx, qn, kn, lam, lam_init, subln_w, rope, need_ctx):
    q, k, v = lat
    qc, kc, vc = ctx

    def heads(t, norm_w, rp):
        t = qk_heads(t, DIFF_HEADS * 2, DIFF_DIM, norm_w, rp)
        return t.reshape(t.shape[0], t.shape[1], DIFF_HEADS, 2, DIFF_DIM)

    def vh(t):
        return t.reshape(t.shape[0], t.shape[1], DIFF_HEADS, DIFF_VDIM)

    def finish(o):
        o = rms_norm(o, subln_w) * (1.0 - lam_init)
        return o.reshape(o.shape[0], o.shape[1], B_V)

    k_ctx = heads(kc, kn, None)
    v_ctx = vh(vc)
    k_all = jnp.concatenate([heads(k, kn, rope), k_ctx], axis=1)
    v_all = jnp.concatenate([vh(v), v_ctx], axis=1)
    y = finish(diff_attend(heads(q, qn, rope), k_all, v_all, lam))
    yc = finish(diff_attend(heads(qc, qn, None), k_ctx, v_ctx, lam)) if need_ctx else None
    return y, yc


def short_conv(x, w):
    ch = x.shape[-1]
    pad = GDN_CONV // 2
    return lax.conv_general_dilated(x, w[:, None, :].astype(x.dtype), window_strides=(1,),
                                    padding=[(pad, pad)], dimension_numbers=('NWC', 'WIO', 'NWC'),
                                    feature_group_count=ch)


def gdn_inputs(q, k, v, bt, a, conv_w, a_log, dt_bias):
    b, n = q.shape[:2]
    qkv = jax.nn.silu(short_conv(jnp.concatenate([q, k, v], axis=-1), conv_w))
    q, k, v = jnp.split(qkv, [C_QK, 2 * C_QK], axis=-1)
    q = l2_norm(q.reshape(b, n, GDN_HEADS, GDN_DK)) * (GDN_DK ** -0.5)
    k = l2_norm(k.reshape(b, n, GDN_HEADS, GDN_DK))
    v = v.reshape(b, n, GDN_HEADS, GDN_DV).astype(jnp.float32)
    beta = jax.nn.sigmoid(bt.astype(jnp.float32).reshape(b, n, 2, GDN_HEADS))
    g = -jnp.exp(a_log.astype(jnp.float32)) * jax.nn.softplus(
        a.astype(jnp.float32).reshape(b, n, 2, GDN_HEADS) + dt_bias.astype(jnp.float32))
    return q, k, v, g, beta


def gdn_chunked(q, k, v, g, beta, s0):
    b, t, h, dk = q.shape
    dv = v.shape[-1]
    nc, L = t // GDN_CHUNK, GDN_CHUNK

    def to_chunks(x):
        x = x.reshape((b, nc, L) + x.shape[2:])
        return jnp.moveaxis(x, 2, 3)

    q, k, v, g, beta = (to_chunks(z) for z in (q, k, v, g, beta))
    gc = jnp.cumsum(g, axis=-1)
    tril = jnp.tril(jnp.ones((L, L), dtype=bool))
    diff = gc[..., :, None] - gc[..., None, :]
    decay = jnp.where(tril, jnp.exp(jnp.where(tril, diff, 0.0)), 0.0)
    kb = k * beta[..., None]
    a_mat = jnp.tril(jnp.einsum('bnhid,bnhjd->bnhij', kb, k) * decay, -1)
    eye = jnp.eye(L, dtype=a_mat.dtype)
    rhs = jnp.concatenate([v * beta[..., None], kb * jnp.exp(gc)[..., None]], axis=-1)
    sol = lax.linalg.triangular_solve(a_mat + eye, rhs, left_side=True, lower=True, unit_diagonal=True)
    u, w = sol[..., :dv], sol[..., dv:]
    qk = jnp.einsum('bnhid,bnhjd->bnhij', q, k) * decay
    q_dec = q * jnp.exp(gc)[..., None]
    k_dec = k * jnp.exp(gc[..., -1:] - gc)[..., None]
    g_last = jnp.exp(gc[..., -1])

    def step(s, xs):
        qd, kd, ui, wi, qki, gl = xs
        v_new = ui - jnp.einsum('bhlk,bhkv->bhlv', wi, s)
        o = jnp.einsum('bhlk,bhkv->bhlv', qd, s) + jnp.einsum('bhij,bhjv->bhiv', qki, v_new)
        s = s * gl[..., None, None] + jnp.einsum('bhlk,bhlv->bhkv', kd, v_new)
        return s, o

    xs = tuple(jnp.moveaxis(z, 1, 0) for z in (q_dec, k_dec, u, w, qk, g_last))
    s_fin, o = lax.scan(step, s0, xs)
    o = jnp.transpose(o, (1, 0, 3, 2, 4)).reshape(b, t, h, dv)
    return o, s_fin


def gdn_branch(lat, ctx, conv_w, a_log, dt_bias, norm_w):
    q, k, v, z, bt, a = lat
    qc, kc, vc, zc, btc, ac = ctx
    b, n = q.shape[:2]
    li = gdn_inputs(q, k, v, bt, a, conv_w, a_log, dt_bias)
    ci = gdn_inputs(qc, kc, vc, btc, ac, conv_w, a_log, dt_bias)
    s0 = jnp.zeros((b, GDN_HEADS, GDN_DK, GDN_DV), jnp.float32)
    o_lat = jnp.zeros((b, n, GDN_HEADS, GDN_DV), jnp.float32)
    o_ctx = jnp.zeros((b, qc.shape[1], GDN_HEADS, GDN_DV), jnp.float32)
    for d in range(2):
        def orient(t):
            return jnp.flip(t, axis=1) if d == 1 else t

        def args(ins):
            qq, kk, vv, gg, bb = ins
            return orient(qq), orient(kk), orient(vv), orient(gg[:, :, d]), orient(bb[:, :, d])

        oc, s_ctx = gdn_chunked(*args(ci), s0)
        ol, _ = gdn_chunked(*args(li), s_ctx)
        o_lat = o_lat + orient(ol)
        o_ctx = o_ctx + orient(oc)

    def finish(o, zz):
        zz = zz.reshape(o.shape).astype(jnp.float32)
        return (rms_norm(o, norm_w) * jax.nn.silu(zz)).reshape(o.shape[0], o.shape[1], C_V).astype(q.dtype)

    return finish(o_lat, z), finish(o_ctx, zc)


def gated_merge(h, ys, w_mgate, w_branch, w_o):
    merged = jnp.zeros(h.shape, h.dtype)
    for i, y in enumerate(ys):
        gate = jax.nn.sigmoid(jnp.einsum('bnd,de->bne', h, w_mgate[i]))
        merged = merged + gate * jnp.einsum('bnw,wd->bnd', y, w_branch[i])
    return jnp.einsum('bnd,de->bne', merged, w_o)


def hybrid_mixer(h, hc, rope, w_in, gqa_qn, gqa_kn, diff_qn, diff_kn, lam, lam_init, diff_subln,
                 conv_w, a_log, dt_bias, gdn_norm_w, w_mgate, w_branch, w_o, need_ctx):
    splits = [int(s) for s in np.cumsum(IN_WIDTHS)[:-1]]
    p = jnp.split(jnp.einsum('bnd,dc->bnc', h, w_in), splits, axis=-1)
    pc = jnp.split(jnp.einsum('bnd,dc->bnc', hc, w_in), splits, axis=-1)
    ya, yac = gqa_branch(p[0:3], pc[0:3], gqa_qn, gqa_kn, rope, need_ctx)
    yb, ybc = diff_branch(p[3:6], pc[3:6], diff_qn, diff_kn, lam, lam_init, diff_subln, rope, need_ctx)
    yg, ygc = gdn_branch(p[6:12], pc[6:12], conv_w, a_log, dt_bias, gdn_norm_w)
    y = gated_merge(h, (ya, yb, yg), w_mgate, w_branch, w_o)
    yc = gated_merge(hc, (yac, ybc, ygc), w_mgate, w_branch, w_o) if need_ctx else None
    return y, yc


def expert_choice_ffn(h, w_router, w_gate, w_up, w_down):
    b, n, d = h.shape
    cap = CAPACITY_FACTOR * n // N_EXPERTS
    logits = jnp.einsum('bnd,de->ben', h, w_router, preferred_element_type=jnp.float32)
    aff = jax.nn.softmax(logits, axis=1)
    top_w, top_i = lax.top_k(aff, cap)
    xin = jax.vmap(lambda hb, ib: hb[ib])(h, top_i)
    hid = jax.nn.silu(jnp.einsum('becd,edf->becf', xin, w_gate)) * jnp.einsum('becd,edf->becf', xin, w_up)
    y = jnp.einsum('becf,efd->becd', hid, w_down) * top_w[..., None].astype(h.dtype)

    def scatter(ib, yb):
        return jnp.zeros((n, d), y.dtype).at[ib.reshape(-1)].add(yb.reshape(-1, d))

    return jax.vmap(scatter)(top_i, y)


def setup_inputs(seed: int = 0) -> dict:
    key = jax.random.key(seed)
    ks = jax.random.split(key, 32)
    D = D_MODEL
    f32 = jnp.float32

    def nrm(i, shape, scale):
        return jax.random.normal(ks[i], shape, f32) * scale

    def gain(i, shape):
        return 1.0 + 0.05 * jax.random.normal(ks[i], shape, f32)

    dt = jnp.exp(jax.random.uniform(ks[20], (DEPTH, 2, GDN_HEADS), f32, math.log(1e-3), math.log(1e-1)))
    return {
        'x': nrm(0, (BATCH, SEQ, D), 1.0),
        'c': nrm(1, (BATCH, D), 1.0),
        'ctx': nrm(2, (BATCH, CTX_LEN, D), 1.0),
        'c_ctx': nrm(3, (D,), 1.0),
        'w_mod': nrm(4, (DEPTH, D, 6 * D), 0.5 * D ** -0.5),
        'b_mod': nrm(5, (DEPTH, 6 * D), 0.01),
        'norm1_w': gain(6, (DEPTH, D)),
        'norm2_w': gain(7, (DEPTH, D)),
        'w_in': nrm(8, (DEPTH, D, IN_COLS), D ** -0.5),
        'gqa_q_norm': gain(9, (DEPTH, HEAD_DIM)),
        'gqa_k_norm': gain(10, (DEPTH, HEAD_DIM)),
        'diff_q_norm': gain(11, (DEPTH, DIFF_DIM)),
        'diff_k_norm': gain(12, (DEPTH, DIFF_DIM)),
        'diff_lambda_q1': nrm(13, (DEPTH, DIFF_DIM), 0.1),
        'diff_lambda_k1': nrm(14, (DEPTH, DIFF_DIM), 0.1),
        'diff_lambda_q2': nrm(15, (DEPTH, DIFF_DIM), 0.1),
        'diff_lambda_k2': nrm(16, (DEPTH, DIFF_DIM), 0.1),
        'diff_subln': gain(17, (DEPTH, DIFF_VDIM)),
        'gdn_conv_w': nrm(18, (DEPTH, GDN_CONV, GDN_CONV_CH), GDN_CONV ** -0.5),
        'gdn_a_log': jnp.log(jax.random.uniform(ks[19], (DEPTH, 2, GDN_HEADS), f32, 1.0, 16.0)),
        'gdn_dt_bias': dt + jnp.log(-jnp.expm1(-dt)),
        'gdn_norm_w': gain(21, (DEPTH, GDN_DV)),
        'w_merge_gate': nrm(22, (DEPTH, N_BRANCH, D, D), D ** -0.5),
        'w_branch': nrm(23, (DEPTH, N_BRANCH, BRANCH_W, D), BRANCH_W ** -0.5),
        'w_out': nrm(24, (DEPTH, D, D), D ** -0.5),
        'w_router': nrm(25, (DEPTH, D, N_EXPERTS), D ** -0.5),
        'w_exp_gate': nrm(26, (DEPTH, N_EXPERTS, D, EXPERT_FF), D ** -0.5),
        'w_exp_up': nrm(27, (DEPTH, N_EXPERTS, D, EXPERT_FF), D ** -0.5),
        'w_exp_down': nrm(28, (DEPTH, N_EXPERTS, EXPERT_FF, D), EXPERT_FF ** -0.5),
    }


def reference(x, c, ctx, c_ctx, w_mod, b_mod, norm1_w, norm2_w, w_in, gqa_q_norm, gqa_k_norm,
              diff_q_norm, diff_k_norm, diff_lambda_q1, diff_lambda_k1, diff_lambda_q2, diff_lambda_k2,
              diff_subln, gdn_conv_w, gdn_a_log, gdn_dt_bias, gdn_norm_w, w_merge_gate, w_branch, w_out,
              w_router, w_exp_gate, w_exp_up, w_exp_down):
    n = x.shape[1]
    rope = axial_rope(n, HEAD_DIM)
    c_act = jax.nn.silu(c)
    cc_act = jax.nn.silu(c_ctx)
    xc = ctx
    for li in range(DEPTH):
        last = li == DEPTH - 1
        mod = (c_act @ w_mod[li] + b_mod[li])[:, None, :]
        modc = cc_act @ w_mod[li] + b_mod[li]
        sh1, sc1, g1, sh2, sc2, g2 = jnp.split(mod, 6, axis=-1)
        sh1c, sc1c, g1c, sh2c, sc2c, g2c = jnp.split(modc, 6, axis=-1)
        lam_init = lambda_init(li)
        lam = (jnp.exp(jnp.sum(diff_lambda_q1[li] * diff_lambda_k1[li]).astype(jnp.float32))
               - jnp.exp(jnp.sum(diff_lambda_q2[li] * diff_lambda_k2[li]).astype(jnp.float32)) + lam_init)

        h = rms_norm(x, norm1_w[li]) * (1.0 + sc1) + sh1
        hc = rms_norm(xc, norm1_w[li]) * (1.0 + sc1c) + sh1c
        y, yc = hybrid_mixer(h, hc, rope, w_in[li], gqa_q_norm[li], gqa_k_norm[li], diff_q_norm[li],
                             diff_k_norm[li], lam, lam_init, diff_subln[li], gdn_conv_w[li], gdn_a_log[li],
                             gdn_dt_bias[li], gdn_norm_w[li], w_merge_gate[li], w_branch[li], w_out[li],
                             not last)
        x = x + g1 * y
        h2 = rms_norm(x, norm2_w[li]) * (1.0 + sc2) + sh2
        x = x + g2 * expert_choice_ffn(h2, w_router[li], w_exp_gate[li], w_exp_up[li], w_exp_down[li])
        if not last:
            xc = xc + g1c * yc
            h2c = rms_norm(xc, norm2_w[li]) * (1.0 + sc2c) + sh2c
            xc = xc + g2c * expert_choice_ffn(h2c, w_router[li], w_exp_gate[li], w_exp_up[li], w_exp_down[li])
    return x
```

```python
import functools
import math

import jax
import jax.numpy as jnp
from jax import lax
from jax.experimental import pallas as pl
from jax.experimental.pallas import tpu as pltpu

F32 = jnp.float32
BF16 = jnp.bfloat16

LANES = 128
TM = 256
GRID_W = 64
HEAD_DIM = 64
GQA_HEADS = 8
GQA_KV_HEADS = 2
DIFF_HEADS = 4
GDN_HEADS = 4
GDN_DK = 128
GDN_CHUNK = 64
N_EXPERTS = 16
CAPACITY_FACTOR = 2
ROPE_THETA = 10000.0
EPS = 1e-6
MOD_ROWS = 16
VMEM_LIMIT = 56 * 1024 * 1024


def _cparams(sem):
    return pltpu.CompilerParams(dimension_semantics=sem, vmem_limit_bytes=VMEM_LIMIT)


def _mm(a, b):
    return jnp.dot(a.astype(BF16), b.astype(BF16), preferred_element_type=F32)


def _mm_nt(a, b):
    return lax.dot_general(a.astype(BF16), b.astype(BF16), (((1,), (1,)), ((), ())),
                           preferred_element_type=F32)


def _split2(a):
    hi = a.astype(BF16)
    lo = (a - hi.astype(F32)).astype(BF16)
    return hi, lo


def _mm3(a, b):
    ah, al = _split2(a)
    bh, bl = _split2(b)
    return _mm(ah, bh) + _mm(ah, bl) + _mm(al, bh)


def _mm_exact_lhs(a_bf16, b):
    b1 = b.astype(BF16)
    r1 = b - b1.astype(F32)
    b2 = r1.astype(BF16)
    b3 = (r1 - b2.astype(F32)).astype(BF16)
    return _mm(a_bf16, b1) + _mm(a_bf16, b2) + _mm(a_bf16, b3)


def _sigmoid(x):
    return 1.0 / (1.0 + jnp.exp(-x))


def _silu(x):
    return x * _sigmoid(x)


def _mod_kernel(a_ref, w_ref, b_ref, o_ref):
    a = _silu(a_ref[...])
    o_ref[0] = _mm3(a, w_ref[0]) + b_ref[0]


def _modulation(cc, w_mod, b_mod):
    depth, d, d6 = w_mod.shape
    nj = d6 // d
    return pl.pallas_call(
        _mod_kernel,
        grid=(depth, nj),
        in_specs=[pl.BlockSpec((MOD_ROWS, d), lambda l, j: (0, 0)),
                  pl.BlockSpec((1, d, d), lambda l, j: (l, 0, j)),
                  pl.BlockSpec((1, 1, d), lambda l, j: (l, 0, j))],
        out_specs=pl.BlockSpec((1, MOD_ROWS, d), lambda l, j: (l, 0, j)),
        out_shape=jax.ShapeDtypeStruct((depth, MOD_ROWS, d6), F32),
        compiler_params=_cparams(("parallel", "parallel")),
        name="modulation",
    )(cc, w_mod, b_mod.reshape(depth, 1, d6))


def _mod_row(mod_ref, nb, ctx_tiles):
    b = pl.program_id(0)
    t = pl.program_id(1)
    row = jnp.where(t < ctx_tiles, nb, b)
    return mod_ref[pl.ds(row, 1), :]


def _rms(x, w):
    return x * lax.rsqrt(jnp.mean(x * x, axis=-1, keepdims=True) + EPS) * w


def _half_norm_rope(t, w, bd, cos, sin):
    ssq = _mm(t * t, bd)
    tn = t * lax.rsqrt(ssq * (1.0 / HEAD_DIM) + EPS) * w
    lane = lax.broadcasted_iota(jnp.int32, tn.shape, 1)
    partner = jnp.where((lane % 32) < 16, pltpu.roll(tn, LANES - 16, axis=1), pltpu.roll(tn, 16, axis=1))
    return tn * cos + partner * sin


def _inproj_kernel(x_ref, mod_ref, nw_ref, w_ref, cos_ref, sin_ref, qkn_ref, bd_ref,
                   h_ref, qa_ref, ka_ref, va_ref, qb_ref, kb_ref, vb_ref, c_ref, z_ref, g_ref,
                   *, nb, ctx_tiles, d):
    m = _mod_row(mod_ref, nb, ctx_tiles)
    sh, sc = m[:, 0:d], m[:, d:2 * d]
    h = _rms(x_ref[0], nw_ref[...]) * (1.0 + sc) + sh
    hb = h.astype(BF16)
    h_ref[0] = hb
    cos, sin, bd = cos_ref[...], sin_ref[...], bd_ref[...]

    def proj(lo, width):
        return jnp.dot(hb, w_ref[:, lo:lo + width], preferred_element_type=F32)

    lane = lax.broadcasted_iota(jnp.int32, (TM, LANES), 1)
    low = lane < HEAD_DIM
    for g in range(4):
        t = proj(g * LANES, LANES)
        qa_ref[0, :, g * LANES:(g + 1) * LANES] = _half_norm_rope(t, qkn_ref[0:1, :], bd, cos, sin).astype(BF16)
    k = _half_norm_rope(proj(512, LANES), qkn_ref[1:2, :], bd, cos, sin)
    kr = pltpu.roll(k, HEAD_DIM, axis=1)
    ka_ref[0, 0] = jnp.where(low, k, kr).astype(BF16)
    ka_ref[0, 1] = jnp.where(low, kr, k).astype(BF16)
    v = proj(640, LANES)
    va_ref[0, 0] = jnp.where(low, v, 1.0).astype(BF16)
    va_ref[0, 1] = jnp.where(low, pltpu.roll(v, HEAD_DIM, axis=1), 1.0).astype(BF16)
    for g in range(4):
        t = proj(768 + g * LANES, LANES)
        qb_ref[0, :, g * LANES:(g + 1) * LANES] = _half_norm_rope(t, qkn_ref[2:3, :], bd, cos, sin).astype(BF16)
        t = proj(1280 + g * LANES, LANES)
        kb_ref[0, :, g * LANES:(g + 1) * LANES] = _half_norm_rope(t, qkn_ref[3:4, :], bd, cos, sin).astype(BF16)
        v = proj(1792 + g * LANES, LANES)
        vb_ref[0, g, :, 0:LANES] = v.astype(BF16)
        vb_ref[0, g, :, LANES:2 * LANES] = jnp.ones((TM, LANES), BF16)
    for g in range(3):
        c_ref[0, :, g * 512:(g + 1) * 512] = proj(2304 + g * 512, 512)
    z_ref[0] = proj(3840, 512).astype(BF16)
    g_ref[0] = proj(4352, LANES)


def _inproj(xs, mod, nw, w_in_p, cos_t, sin_t, qkn, bd, *, nb, ctx_tiles):
    b, t, d = xs.shape
    nt = t // TM
    tile = lambda width: pl.BlockSpec((1, TM, width), lambda i, j: (i, j, 0))
    whole = lambda a: pl.BlockSpec(a.shape, lambda i, j: (0,) * a.ndim)
    out_shape = (
        jax.ShapeDtypeStruct((b, t, d), BF16),
        jax.ShapeDtypeStruct((b, t, 512), BF16),
        jax.ShapeDtypeStruct((b, 2, t, LANES), BF16),
        jax.ShapeDtypeStruct((b, 2, t, LANES), BF16),
        jax.ShapeDtypeStruct((b, t, 512), BF16),
        jax.ShapeDtypeStruct((b, t, 512), BF16),
        jax.ShapeDtypeStruct((b, 4, t, 2 * LANES), BF16),
        jax.ShapeDtypeStruct((b, t, 1536), F32),
        jax.ShapeDtypeStruct((b, t, 512), BF16),
        jax.ShapeDtypeStruct((b, t, LANES), F32),
    )
    out_specs = (
        tile(d), tile(512),
        pl.BlockSpec((1, 2, TM, LANES), lambda i, j: (i, 0, j, 0)),
        pl.BlockSpec((1, 2, TM, LANES), lambda i, j: (i, 0, j, 0)),
        tile(512), tile(512),
        pl.BlockSpec((1, 4, TM, 2 * LANES), lambda i, j: (i, 0, j, 0)),
        tile(1536), tile(512), tile(LANES),
    )
    return pl.pallas_call(
        functools.partial(_inproj_kernel, nb=nb, ctx_tiles=ctx_tiles, d=d),
        grid=(b, nt),
        in_specs=[tile(d), whole(mod), whole(nw), whole(w_in_p),
                  pl.BlockSpec((TM, LANES), lambda i, j: (j, 0)),
                  pl.BlockSpec((TM, LANES), lambda i, j: (j, 0)),
                  whole(qkn), whole(bd)],
        out_specs=out_specs,
        out_shape=out_shape,
        compiler_params=_cparams(("parallel", "parallel")),
        name="inproj",
    )(xs, mod, nw, w_in_p, cos_t, sin_t, qkn, bd)


def _attend(qs, k_at, v_at, nk, vw):
    mrows = qs.shape[0]

    def body(i, carry):
        mx, acc = carry
        s = _mm_nt(qs, k_at(i))
        mnew = jnp.maximum(mx, jnp.max(s, axis=-1, keepdims=True))
        p = jnp.exp(s - mnew)
        acc = acc * jnp.exp(mx - mnew) + _mm(p, v_at(i))
        return mnew, acc

    init = (jnp.full((mrows, 1), -jnp.inf, F32), jnp.zeros((mrows, vw), F32))
    return lax.fori_loop(0, nk, body, init)[1]


def _gqa_kernel(q_ref, k_ref, v_ref, o_ref, *, ctx_tiles, nt):
    t = pl.program_id(2)
    nk = jnp.where(t < ctx_tiles, ctx_tiles, nt)
    q = q_ref[0]
    lane = lax.broadcasted_iota(jnp.int32, q.shape, 1)
    low = lane < HEAD_DIM
    zero = jnp.zeros_like(q)
    qs = jnp.concatenate([jnp.where(low, q, zero), jnp.where(low, zero, q)], axis=0)
    acc = _attend(qs, lambda i: k_ref[0, 0, pl.ds(pl.multiple_of(i * TM, TM), TM), :],
                  lambda i: v_ref[0, 0, pl.ds(pl.multiple_of(i * TM, TM), TM), :], nk, LANES)
    o = acc / acc[:, HEAD_DIM:HEAD_DIM + 1]
    o_ref[0] = jnp.where(low, o[:TM], pltpu.roll(o[TM:], HEAD_DIM, axis=1)).astype(BF16)


def _gqa_attention(qa, ka, va, *, ctx_tiles):
    b, t, _ = qa.shape
    nt = t // TM
    return pl.pallas_call(
        functools.partial(_gqa_kernel, ctx_tiles=ctx_tiles, nt=nt),
        grid=(b, 4, nt),
        in_specs=[pl.BlockSpec((1, TM, LANES), lambda i, g, j: (i, j, g)),
                  pl.BlockSpec((1, 1, t, LANES), lambda i, g, j: (i, g // 2, 0, 0)),
                  pl.BlockSpec((1, 1, t, LANES), lambda i, g, j: (i, g // 2, 0, 0))],
        out_specs=pl.BlockSpec((1, TM, LANES), lambda i, g, j: (i, j, g)),
        out_shape=jax.ShapeDtypeStruct((b, t, 512), BF16),
        compiler_params=_cparams(("parallel", "parallel", "parallel")),
        name="gqa_attention",
    )(qa, ka, va)


def _diff_kernel(q_ref, k_ref, v_ref, lam_ref, sw_ref, o_ref, *, ctx_tiles, nt, lam_init):
    t = pl.program_id(2)
    nk = jnp.where(t < ctx_tiles, ctx_tiles, nt)
    q = q_ref[0]
    lane = lax.broadcasted_iota(jnp.int32, q.shape, 1)
    low = lane < HEAD_DIM
    zero = jnp.zeros_like(q)
    qs = jnp.concatenate([jnp.where(low, q, zero), jnp.where(low, zero, q)], axis=0)
    acc = _attend(qs, lambda i: k_ref[0, pl.ds(pl.multiple_of(i * TM, TM), TM), :],
                  lambda i: v_ref[0, 0, pl.ds(pl.multiple_of(i * TM, TM), TM), :], nk, 2 * LANES)
    prod = lam_ref[0:1, :] * lam_ref[1:2, :]
    l1 = jnp.sum(jnp.where(low[0:1], prod, 0.0), axis=-1, keepdims=True)
    l2 = jnp.sum(jnp.where(low[0:1], 0.0, prod), axis=-1, keepdims=True)
    lam = jnp.exp(l1) - jnp.exp(l2) + lam_init
    o1 = acc[:TM, :LANES] / acc[:TM, LANES:LANES + 1]
    o2 = acc[TM:, :LANES] / acc[TM:, LANES:LANES + 1]
    o = o1 - lam * o2
    o_ref[0] = (_rms(o, sw_ref[...]) * (1.0 - lam_init)).astype(BF16)


def _diff_attention(qb, kb, vb, lam_rows, subln, *, ctx_tiles, lam_init):
    b, t, _ = qb.shape
    nt = t // TM
    return pl.pallas_call(
        functools.partial(_diff_kernel, ctx_tiles=ctx_tiles, nt=nt, lam_init=lam_init),
        grid=(b, DIFF_HEADS, nt),
        in_specs=[pl.BlockSpec((1, TM, LANES), lambda i, g, j: (i, j, g)),
                  pl.BlockSpec((1, t, LANES), lambda i, g, j: (i, 0, g)),
                  pl.BlockSpec((1, 1, t, 2 * LANES), lambda i, g, j: (i, g, 0, 0)),
                  pl.BlockSpec(lam_rows.shape, lambda i, g, j: (0, 0)),
                  pl.BlockSpec(subln.shape, lambda i, g, j: (0, 0))],
        out_specs=pl.BlockSpec((1, TM, LANES), lambda i, g, j: (i, j, g)),
        out_shape=jax.ShapeDtypeStruct((b, t, 512), BF16),
        compiler_params=_cparams(("parallel", "parallel", "parallel")),
        name="diff_attention",
    )(qb, kb, vb, lam_rows, subln)


def _gdn_prep_kernel(c_ref, cp_ref, cn_ref, g_ref, cw_ref, gp_ref, q_ref, k_ref, v_ref, bg_ref, *, ctx_tiles, nt):
    t = pl.program_id(1)
    x = c_ref[0]
    first = jnp.logical_or(t == 0, t == ctx_tiles)
    last = jnp.logical_or(t == ctx_tiles - 1, t == nt - 1)
    prev_row = jnp.where(first, 0.0, cp_ref[0, 7:8, :])
    next_row = jnp.where(last, 0.0, cn_ref[0, 0:1, :])
    ridx = lax.broadcasted_iota(jnp.int32, x.shape, 0)
    xm = jnp.where(ridx == 0, prev_row, pltpu.roll(x, 1, axis=0))
    xp = jnp.where(ridx == TM - 1, next_row, pltpu.roll(x, TM - 1, axis=0))
    y = _silu(xm * cw_ref[0:1, :] + x * cw_ref[1:2, :] + xp * cw_ref[2:3, :])
    for h in range(GDN_HEADS):
        sl = slice(h * GDN_DK, (h + 1) * GDN_DK)
        qh = y[:, sl]
        q_ref[0, :, sl] = (qh * lax.rsqrt(jnp.sum(qh * qh, axis=-1, keepdims=True) + EPS)
                           * (GDN_DK ** -0.5)).astype(BF16)
        kh = y[:, 512 + h * GDN_DK:512 + (h + 1) * GDN_DK]
        k_ref[0, :, sl] = (kh * lax.rsqrt(jnp.sum(kh * kh, axis=-1, keepdims=True) + EPS)).astype(BF16)
    v_ref[0] = y[:, 1024:1536].astype(BF16)
    gx = g_ref[0]
    lane = lax.broadcasted_iota(jnp.int32, gx.shape, 1)
    a = gx + gp_ref[1:2, :]
    softplus = jnp.maximum(a, 0.0) + jnp.log(1.0 + jnp.exp(-jnp.abs(a)))
    gdec = -jnp.exp(gp_ref[0:1, :]) * softplus
    bg_ref[0] = jnp.where(lane < 8, _sigmoid(gx), jnp.where(lane < 16, gdec, 0.0))


def _gdn_prep(cqkv, gates, conv_w8, gparams, *, ctx_tiles):
    b, t, cch = cqkv.shape
    nt = t // TM
    r8 = TM // 8
    tile = lambda width: pl.BlockSpec((1, TM, width), lambda i, j: (i, j, 0))
    return pl.pallas_call(
        functools.partial(_gdn_prep_kernel, ctx_tiles=ctx_tiles, nt=nt),
        grid=(b, nt),
        in_specs=[tile(cch),
                  pl.BlockSpec((1, 8, cch), lambda i, j: (i, jnp.maximum(j * r8 - 1, 0), 0)),
                  pl.BlockSpec((1, 8, cch), lambda i, j: (i, jnp.minimum((j + 1) * r8, nt * r8 - 1), 0)),
                  tile(LANES),
                  pl.BlockSpec(conv_w8.shape, lambda i, j: (0, 0)),
                  pl.BlockSpec(gparams.shape, lambda i, j: (0, 0))],
        out_specs=(tile(512), tile(512), tile(512), tile(LANES)),
        out_shape=(jax.ShapeDtypeStruct((b, t, 512), BF16), jax.ShapeDtypeStruct((b, t, 512), BF16),
                   jax.ShapeDtypeStruct((b, t, 512), BF16), jax.ShapeDtypeStruct((b, t, LANES), F32)),
        compiler_params=_cparams(("parallel", "parallel")),
        name="gdn_prep",
    )(cqkv, cqkv, cqkv, gates, conv_w8, gparams)


def _unit_tri_inverse(a):
    n = a.shape[0]
    r = lax.broadcasted_iota(jnp.int32, (n, n), 0)
    c = lax.broadcasted_iota(jnp.int32, (n, n), 1)
    p = -a
    t = jnp.where(r == c, 1.0, 0.0) + p
    for _ in range(5):
        p = _mm3(p, p)
        t = t + _mm3(t, p)
    return t


def _gdn_chunk_kernel(q_ref, k_ref, v_ref, bg_ref, l1_ref, l2_ref, u_ref, gl_ref):
    L = GDN_CHUNK
    r = lax.broadcasted_iota(jnp.int32, (L, L), 0)
    c = lax.broadcasted_iota(jnp.int32, (L, L), 1)
    bg = bg_ref[0]
    for d in range(2):
        incl = (r >= c) if d == 0 else (r <= c)
        strict = (r > c) if d == 0 else (r < c)
        gc = _mm_exact_lhs(jnp.where(incl, 1.0, 0.0).astype(BF16), bg)
        gct = gc.T
        last = L - 1 if d == 0 else 0
        gl_rows = []
        for h in range(GDN_HEADS):
            col = 8 + 4 * d + h
            gcc = gc[:, col:col + 1]
            gcr = gct[col:col + 1, :]
            dec = jnp.where(incl, jnp.exp(jnp.where(incl, gcc - gcr, 0.0)), 0.0)
            beta = bg[:, 4 * d + h:4 * d + h + 1]
            sl = slice(h * GDN_DK, (h + 1) * GDN_DK)
            qh, kh = q_ref[0, :, sl], k_ref[0, :, sl]
            kf = kh.astype(F32)
            kbeta = kf * beta
            a = jnp.where(strict, _mm_nt(kbeta, kh) * dec, 0.0)
            tinv = _unit_tri_inverse(a)
            eg = jnp.exp(gcc)
            rhs = jnp.concatenate([v_ref[0, :, sl].astype(F32) * beta, kbeta * eg], axis=1)
            sol = _mm3(tinv, rhs)
            qk = _mm_nt(qh, kh) * dec
            glast = gcc[last:last + 1, :]
            kdec = kf * jnp.exp(glast - gcc)
            u_ref[0, d, 0, h] = sol[:, :GDN_DK]
            l1_ref[0, d, 0, h, 0:L, :] = sol[:, GDN_DK:].astype(BF16)
            l1_ref[0, d, 0, h, L:2 * L, :] = (qh.astype(F32) * eg).astype(BF16)
            l2_ref[0, d, 0, h, 0:L, :] = qk.astype(BF16)
            l2_ref[0, d, 0, h, L:, :] = kdec.T.astype(BF16)
            gl_rows.append(jnp.broadcast_to(jnp.exp(glast), (1, LANES)))
        gl_ref[0, d, 0] = jnp.concatenate(gl_rows + [jnp.zeros((4, LANES), F32)], axis=0)


def _gdn_chunks(qg, kg, vg, bg):
    b, t, _ = qg.shape
    nc = t // GDN_CHUNK
    L = GDN_CHUNK
    tile = lambda width: pl.BlockSpec((1, L, width), lambda i, j: (i, j, 0))
    return pl.pallas_call(
        _gdn_chunk_kernel,
        grid=(b, nc),
        in_specs=[tile(512), tile(512), tile(512), tile(LANES)],
        out_specs=(pl.BlockSpec((1, 2, 1, 4, 2 * L, GDN_DK), lambda i, j: (i, 0, j, 0, 0, 0)),
                   pl.BlockSpec((1, 2, 1, 4, L + GDN_DK, L), lambda i, j: (i, 0, j, 0, 0, 0)),
                   pl.BlockSpec((1, 2, 1, 4, L, GDN_DK), lambda i, j: (i, 0, j, 0, 0, 0)),
                   pl.BlockSpec((1, 2, 1, 8, LANES), lambda i, j: (i, 0, j, 0, 0))),
        out_shape=(jax.ShapeDtypeStruct((b, 2, nc, 4, 2 * L, GDN_DK), BF16),
                   jax.ShapeDtypeStruct((b, 2, nc, 4, L + GDN_DK, L), BF16),
                   jax.ShapeDtypeStruct((b, 2, nc, 4, L, GDN_DK), F32),
                   jax.ShapeDtypeStruct((b, 2, nc, 8, LANES), F32)),
        compiler_params=_cparams(("parallel", "parallel")),
        name="gdn_chunks",
    )(qg, kg, vg, bg)


def _gdn_scan_kernel(l1f_ref, l2f_ref, uf_ref, glf_ref, l1b_ref, l2b_ref, ub_ref, glb_ref,
                     of_ref, ob_ref, s_ref):
    L = GDN_CHUNK

    @pl.when(pl.program_id(1) == 0)
    def _():
        s_ref[...] = jnp.zeros_like(s_ref)

    for d, (l1_ref, l2_ref, u_ref, gl_ref, o_ref) in enumerate(
            ((l1f_ref, l2f_ref, uf_ref, glf_ref, of_ref), (l1b_ref, l2b_ref, ub_ref, glb_ref, ob_ref))):
        for h in range(GDN_HEADS):
            s = s_ref[d * GDN_HEADS + h]
            r1 = _mm(l1_ref[0, 0, 0, h], s)
            v_new = u_ref[0, 0, 0, h] - r1[:L]
            r2 = _mm(l2_ref[0, 0, 0, h], v_new)
            o_ref[0, :, h * GDN_DK:(h + 1) * GDN_DK] = r1[L:] + r2[:L]
            s_ref[d * GDN_HEADS + h] = s * gl_ref[0, 0, 0, h:h + 1, :] + r2[L:]


def _gdn_scan(l1, l2, u, gl, *, ctx_chunks):
    b, _, nc = l1.shape[:3]
    L = GDN_CHUNK

    def rev(j):
        return jnp.where(j < ctx_chunks, ctx_chunks - 1 - j, nc - 1 - (j - ctx_chunks))

    def spec(a, d):
        blk = (1, 1, 1) + a.shape[3:]
        zeros = (0,) * (a.ndim - 3)
        if d == 0:
            return pl.BlockSpec(blk, lambda i, j: (i, 0, j) + zeros)
        return pl.BlockSpec(blk, lambda i, j: (i, 1, rev(j)) + zeros)

    return pl.pallas_call(
        _gdn_scan_kernel,
        grid=(b, nc),
        in_specs=[spec(l1, 0), spec(l2, 0), spec(u, 0), spec(gl, 0),
                  spec(l1, 1), spec(l2, 1), spec(u, 1), spec(gl, 1)],
        out_specs=(pl.BlockSpec((1, L, 512), lambda i, j: (i, j, 0)),
                   pl.BlockSpec((1, L, 512), lambda i, j: (i, rev(j), 0))),
        out_shape=(jax.ShapeDtypeStruct((b, nc * L, 512), F32), jax.ShapeDtypeStruct((b, nc * L, 512), F32)),
        scratch_shapes=[pltpu.VMEM((2 * GDN_HEADS, GDN_DK, GDN_DK), F32)],
        compiler_params=_cparams(("parallel", "arbitrary")),
        name="gdn_scan",
    )(l1, l2, u, gl, l1, l2, u, gl)


def _merge_kernel(x_ref, h_ref, ya_ref, yb_ref, of_ref, ob_ref, z_ref, mod_ref, gnw_ref, n2w_ref,
                  wg_ref, wb_ref, wo_ref, wr_ref, xo_ref, h2_ref, aff_ref, *, nb, ctx_tiles, d):
    m = _mod_row(mod_ref, nb, ctx_tiles)
    g1, sh2, sc2 = m[:, 2 * d:3 * d], m[:, 3 * d:4 * d], m[:, 4 * d:5 * d]
    o = of_ref[0] + ob_ref[0]
    z = z_ref[0].astype(F32)
    ygs = []
    for h in range(GDN_HEADS):
        sl = slice(h * GDN_DK, (h + 1) * GDN_DK)
        ygs.append(_rms(o[:, sl], gnw_ref[...]) * _silu(z[:, sl]))
    yg = jnp.concatenate(ygs, axis=1).astype(BF16)
    hb = h_ref[0]
    merged = jnp.zeros((TM, d), F32)
    for i, y in enumerate((ya_ref[0], yb_ref[0], yg)):
        gate = _sigmoid(jnp.dot(hb, wg_ref[i], preferred_element_type=F32))
        merged = merged + gate * jnp.dot(y, wb_ref[i], preferred_element_type=F32)
    x = x_ref[0] + g1 * jnp.dot(merged.astype(BF16), wo_ref[...], preferred_element_type=F32)
    xo_ref[0] = x
    h2 = _rms(x, n2w_ref[...]) * (1.0 + sc2) + sh2
    h2b = h2.astype(BF16)
    h2_ref[0] = h2b
    logits = _mm3(h2, wr_ref[...])
    lane = lax.broadcasted_iota(jnp.int32, logits.shape, 1)
    logits = jnp.where(lane < N_EXPERTS, logits, -jnp.inf)
    e = jnp.exp(logits - jnp.max(logits, axis=-1, keepdims=True))
    aff_ref[0] = e / jnp.sum(e, axis=-1, keepdims=True)


def _merge(xs, h, ya, yb, o_f, o_b, z, mod, gnw, n2w, wg, wb, wo, wr, *, nb, ctx_tiles):
    b, t, d = xs.shape
    nt = t // TM
    tile = lambda width: pl.BlockSpec((1, TM, width), lambda i, j: (i, j, 0))
    whole = lambda a: pl.BlockSpec(a.shape, lambda i, j: (0,) * a.ndim)
    return pl.pallas_call(
        functools.partial(_merge_kernel, nb=nb, ctx_tiles=ctx_tiles, d=d),
        grid=(b, nt),
        in_specs=[tile(d), tile(d), tile(512), tile(512), tile(512), tile(512), tile(512),
                  whole(mod), whole(gnw), whole(n2w), whole(wg), whole(wb), whole(wo), whole(wr)],
        out_specs=(tile(d), tile(d), tile(LANES)),
        out_shape=(jax.ShapeDtypeStruct((b, t, d), F32), jax.ShapeDtypeStruct((b, t, d), BF16),
                   jax.ShapeDtypeStruct((b, t, LANES), F32)),
        compiler_params=_cparams(("parallel", "parallel")),
        name="merge",
    )(xs, h, ya, yb, o_f, o_b, z, mod, gnw, n2w, wg, wb, wo, wr)


def _expert_kernel(x_ref, tw_ref, wg_ref, wu_ref, wd_ref, o_ref):
    x = x_ref[0, 0]
    hid = _silu(jnp.dot(x, wg_ref[0], preferred_element_type=F32)) * jnp.dot(x, wu_ref[0], preferred_element_type=F32)
    y = jnp.dot(hid.astype(BF16), wd_ref[0], preferred_element_type=F32)
    o_ref[0, 0] = y * tw_ref[0, 0]


def _experts(xin, tw, wg, wu, wd):
    b, e, cap, d = xin.shape
    ff = wg.shape[-1]
    return pl.pallas_call(
        _expert_kernel,
        grid=(e, b),
        in_specs=[pl.BlockSpec((1, 1, cap, d), lambda k, i: (i, k, 0, 0)),
                  pl.BlockSpec((1, 1, cap, 1), lambda k, i: (i, k, 0, 0)),
                  pl.BlockSpec((1, d, ff), lambda k, i: (k, 0, 0)),
                  pl.BlockSpec((1, d, ff), lambda k, i: (k, 0, 0)),
                  pl.BlockSpec((1, ff, d), lambda k, i: (k, 0, 0))],
        out_specs=pl.BlockSpec((1, 1, cap, d), lambda k, i: (i, k, 0, 0)),
        out_shape=jax.ShapeDtypeStruct((b, e, cap, d), F32),
        compiler_params=_cparams(("parallel", "parallel")),
        name="experts",
    )(xin, tw, wg, wu, wd)


def _moe_segment(h2, aff, wg, wu, wd):
    b, n, d = h2.shape
    cap = CAPACITY_FACTOR * n // N_EXPERTS
    top_w, top_i = lax.top_k(jnp.swapaxes(aff, 1, 2), cap)
    xin = jax.vmap(lambda hb, ib: hb[ib])(h2, top_i)
    y = _experts(xin, top_w[..., None], wg, wu, wd)

    def scatter(ib, yb):
        return jnp.zeros((n, d), F32).at[ib.reshape(-1)].add(yb.reshape(-1, d))

    return jax.vmap(scatter)(top_i, y)


def _lambda_init(layer_idx):
    return 0.8 - 0.6 * math.exp(-0.3 * layer_idx)


def _rope_tables(n_lat, n_ctx):
    rows = n_lat // GRID_W
    row = jnp.repeat(jnp.arange(rows, dtype=F32), GRID_W)
    col = jnp.tile(jnp.arange(GRID_W, dtype=F32), rows)
    axis_dim = HEAD_DIM // 2
    inv_freq = ROPE_THETA ** (-jnp.arange(0, axis_dim, 2, dtype=F32) / axis_dim)
    ang_r, ang_c = row[:, None] * inv_freq, col[:, None] * inv_freq
    cos = jnp.concatenate([jnp.cos(ang_r)] * 2 + [jnp.cos(ang_c)] * 2, axis=-1)
    sin = jnp.concatenate([-jnp.sin(ang_r), jnp.sin(ang_r), -jnp.sin(ang_c), jnp.sin(ang_c)], axis=-1)
    cos = jnp.concatenate([jnp.ones((n_ctx, HEAD_DIM), F32), cos], axis=0)
    sin = jnp.concatenate([jnp.zeros((n_ctx, HEAD_DIM), F32), sin], axis=0)
    return jnp.tile(cos, (1, 2)), jnp.tile(sin, (1, 2))


def kernel(x, c, ctx, c_ctx, w_mod, b_mod, norm1_w, norm2_w, w_in, gqa_q_norm, gqa_k_norm, diff_q_norm,
           diff_k_norm, diff_lambda_q1, diff_lambda_k1, diff_lambda_q2, diff_lambda_k2, diff_subln,
           gdn_conv_w, gdn_a_log, gdn_dt_bias, gdn_norm_w, w_merge_gate, w_branch, w_out, w_router,
           w_exp_gate, w_exp_up, w_exp_down):
    nb, n_lat, d = x.shape
    n_ctx = ctx.shape[1]
    depth = w_mod.shape[0]
    assert n_ctx % TM == 0 and n_lat % TM == 0 and nb < MOD_ROWS and d == 1024
    ctx_tiles = n_ctx // TM
    ctx_chunks = n_ctx // GDN_CHUNK

    cos_t, sin_t = _rope_tables(n_lat, n_ctx)
    cc = jnp.zeros((MOD_ROWS, d), F32).at[:nb].set(c).at[nb].set(c_ctx)
    mods = _modulation(cc, w_mod, b_mod)
    half = lax.broadcasted_iota(jnp.int32, (LANES, LANES), 0) // HEAD_DIM
    bd = (half == half.T).astype(BF16)

    xs = jnp.concatenate([ctx, x], axis=1)
    scale = HEAD_DIM ** -0.5
    for li in range(depth):
        lam_init = _lambda_init(li)
        w_in_p = jnp.pad(w_in[li], ((0, 0), (0, 4480 - w_in.shape[-1]))).astype(BF16)
        qkn = jnp.stack([jnp.tile(gqa_q_norm[li] * scale, 2), jnp.tile(gqa_k_norm[li], 2),
                         jnp.tile(diff_q_norm[li] * scale, 2), jnp.tile(diff_k_norm[li], 2)])
        (h, qa, ka, va, qb, kb, vb, cqkv, z, gates) = _inproj(
            xs, mods[li], norm1_w[li][None, :], w_in_p, cos_t, sin_t, qkn, bd, nb=nb, ctx_tiles=ctx_tiles)

        ya = _gqa_attention(qa, ka, va, ctx_tiles=ctx_tiles)
        lam_rows = jnp.stack([jnp.concatenate([diff_lambda_q1[li], diff_lambda_q2[li]]),
                              jnp.concatenate([diff_lambda_k1[li], diff_lambda_k2[li]])])
        yb = _diff_attention(qb, kb, vb, lam_rows, diff_subln[li][None, :], ctx_tiles=ctx_tiles,
                             lam_init=lam_init)

        conv_w8 = jnp.pad(gdn_conv_w[li], ((0, 5), (0, 0)))
        gparams = jnp.zeros((2, LANES), F32)
        gparams = gparams.at[0, 8:16].set(gdn_a_log[li].reshape(-1)).at[1, 8:16].set(gdn_dt_bias[li].reshape(-1))
        qg, kg, vg, bg = _gdn_prep(cqkv, gates, conv_w8, gparams, ctx_tiles=ctx_tiles)
        l1, l2, u, gl = _gdn_chunks(qg, kg, vg, bg)
        o_f, o_b = _gdn_scan(l1, l2, u, gl, ctx_chunks=ctx_chunks)

        wr = jnp.pad(w_router[li], ((0, 0), (0, LANES - N_EXPERTS)))
        xs, h2, aff = _merge(xs, h, ya, yb, o_f, o_b, z, mods[li], gdn_norm_w[li][None, :],
                             norm2_w[li][None, :], w_merge_gate[li].astype(BF16), w_branch[li].astype(BF16),
                             w_out[li].astype(BF16), wr, nb=nb, ctx_tiles=ctx_tiles)

        wg, wu, wd = (w_exp_gate[li].astype(BF16), w_exp_up[li].astype(BF16), w_exp_down[li].astype(BF16))
        aff = aff[:, :, :N_EXPERTS]
        moe = jnp.concatenate([_moe_segment(h2[:, :n_ctx], aff[:, :n_ctx], wg, wu, wd),
                               _moe_segment(h2[:, n_ctx:], aff[:, n_ctx:], wg, wu, wd)], axis=1)
        g2 = jnp.concatenate([jnp.broadcast_to(mods[li, nb, 5 * d:][None, None, :], (nb, n_ctx, d)),
                              jnp.broadcast_to(mods[li, :nb, None, 5 * d:], (nb, n_lat, d))], axis=1)
        xs = xs + g2 * moe
    return xs[:, n_ctx:]
```

```python
import functools
import math

import jax
import jax.numpy as jnp
from jax import lax
from jax.experimental import pallas as pl
from jax.experimental.pallas import tpu as pltpu

F32 = jnp.float32
BF16 = jnp.bfloat16

LANES = 128
TM = 256
GRID_W = 64
HEAD_DIM = 64
GQA_HEADS = 8
GQA_KV_HEADS = 2
DIFF_HEADS = 4
GDN_HEADS = 4
GDN_DK = 128
GDN_CHUNK = 64
N_EXPERTS = 16
CAPACITY_FACTOR = 2
ROPE_THETA = 10000.0
EPS = 1e-6
ONES_ROWS = 16
MOD_ROWS = 16
VMEM_LIMIT = 56 * 1024 * 1024


def _cparams(sem):
    return pltpu.CompilerParams(dimension_semantics=sem, vmem_limit_bytes=VMEM_LIMIT)


def _mm(a, b):
    return jnp.dot(a.astype(BF16), b.astype(BF16), preferred_element_type=F32)


def _mm_nt(a, b):
    return lax.dot_general(a.astype(BF16), b.astype(BF16), (((1,), (1,)), ((), ())),
                           preferred_element_type=F32)


def _split2(a):
    hi = a.astype(BF16)
    lo = (a - hi.astype(F32)).astype(BF16)
    return hi, lo


def _mm3(a, b):
    ah, al = _split2(a)
    bh, bl = _split2(b)
    return _mm(ah, bh) + _mm(ah, bl) + _mm(al, bh)


def _mm_exact_lhs(a_bf16, b):
    b1 = b.astype(BF16)
    r1 = b - b1.astype(F32)
    b2 = r1.astype(BF16)
    b3 = (r1 - b2.astype(F32)).astype(BF16)
    return _mm(a_bf16, b1) + _mm(a_bf16, b2) + _mm(a_bf16, b3)


def _sigmoid(x):
    return 1.0 / (1.0 + jnp.exp(-x))


def _silu(x):
    return x * _sigmoid(x)


def _mod_kernel(a_ref, w_ref, b_ref, o_ref):
    a = _silu(a_ref[...])
    o_ref[0] = _mm3(a, w_ref[0]) + b_ref[0]


def _modulation(cc, w_mod, b_mod):
    depth, d, d6 = w_mod.shape
    nj = d6 // d
    return pl.pallas_call(
        _mod_kernel,
        grid=(depth, nj),
        in_specs=[pl.BlockSpec((MOD_ROWS, d), lambda l, j: (0, 0)),
                  pl.BlockSpec((1, d, d), lambda l, j: (l, 0, j)),
                  pl.BlockSpec((1, 1, d), lambda l, j: (l, 0, j))],
        out_specs=pl.BlockSpec((1, MOD_ROWS, d), lambda l, j: (l, 0, j)),
        out_shape=jax.ShapeDtypeStruct((depth, MOD_ROWS, d6), F32),
        compiler_params=_cparams(("parallel", "parallel")),
        name="modulation",
    )(cc, w_mod, b_mod.reshape(depth, 1, d6))


def _mod_row(mod_ref, nb, ctx_tiles):
    b = pl.program_id(0)
    t = pl.program_id(1)
    row = jnp.where(t < ctx_tiles, nb, b)
    return mod_ref[pl.ds(row, 1), :]


def _rms(x, w):
    return x * lax.rsqrt(jnp.mean(x * x, axis=-1, keepdims=True) + EPS) * w


def _half_norm_rope(t, w, bd, cos, sin):
    ssq = _mm(t * t, bd)
    tn = t * lax.rsqrt(ssq * (1.0 / HEAD_DIM) + EPS) * w
    lane = lax.broadcasted_iota(jnp.int32, tn.shape, 1)
    partner = jnp.where((lane % 32) < 16, pltpu.roll(tn, LANES - 16, axis=1), pltpu.roll(tn, 16, axis=1))
    return tn * cos + partner * sin


def _inproj_kernel(x_ref, mod_ref, nw_ref, w_ref, cos_ref, sin_ref, qkn_ref, bd_ref,
                   h_ref, qa_ref, ka_ref, va_ref, qb_ref, kb_ref, vb_ref, c_ref, z_ref, g_ref,
                   *, nb, ctx_tiles, d):
    m = _mod_row(mod_ref, nb, ctx_tiles)
    sh, sc = m[:, 0:d], m[:, d:2 * d]
    h = _rms(x_ref[0], nw_ref[...]) * (1.0 + sc) + sh
    hb = h.astype(BF16)
    h_ref[0] = hb
    cos, sin, bd = cos_ref[...], sin_ref[...], bd_ref[...]

    def proj(lo, width):
        return jnp.dot(hb, w_ref[:, lo:lo + width], preferred_element_type=F32)

    ones_rows = jnp.ones((ONES_ROWS, TM), BF16)
    for g in range(4):
        t = _half_norm_rope(proj(g * LANES, LANES), qkn_ref[0:1, :], bd, cos, sin)
        qa_ref[0, g * LANES:(g + 1) * LANES, :] = t.T.astype(BF16)
    ka_ref[0] = _half_norm_rope(proj(512, LANES), qkn_ref[1:2, :], bd, cos, sin).astype(BF16)
    vt = proj(640, LANES).T.astype(BF16)
    for j in range(GQA_KV_HEADS):
        va_ref[0, j, 0:HEAD_DIM, :] = vt[j * HEAD_DIM:(j + 1) * HEAD_DIM]
        va_ref[0, j, HEAD_DIM:HEAD_DIM + ONES_ROWS, :] = ones_rows
    for g in range(4):
        t = _half_norm_rope(proj(768 + g * LANES, LANES), qkn_ref[2:3, :], bd, cos, sin)
        qb_ref[0, g * LANES:(g + 1) * LANES, :] = t.T.astype(BF16)
        t = proj(1280 + g * LANES, LANES)
        kb_ref[0, :, g * LANES:(g + 1) * LANES] = _half_norm_rope(t, qkn_ref[3:4, :], bd, cos, sin).astype(BF16)
        vb_ref[0, g, 0:LANES, :] = proj(1792 + g * LANES, LANES).T.astype(BF16)
        vb_ref[0, g, LANES:LANES + ONES_ROWS, :] = ones_rows
    for g in range(3):
        c_ref[0, :, g * 512:(g + 1) * 512] = proj(2304 + g * 512, 512)
    z_ref[0] = proj(3840, 512).astype(BF16)
    g_ref[0] = proj(4352, LANES)


def _inproj(xs, mod, nw, w_in_p, cos_t, sin_t, qkn, bd, *, nb, ctx_tiles):
    b, t, d = xs.shape
    nt = t // TM
    tile = lambda width: pl.BlockSpec((1, TM, width), lambda i, j: (i, j, 0))
    whole = lambda a: pl.BlockSpec(a.shape, lambda i, j: (0,) * a.ndim)
    out_shape = (
        jax.ShapeDtypeStruct((b, t, d), BF16),
        jax.ShapeDtypeStruct((b, 512, t), BF16),
        jax.ShapeDtypeStruct((b, t, LANES), BF16),
        jax.ShapeDtypeStruct((b, 2, HEAD_DIM + ONES_ROWS, t), BF16),
        jax.ShapeDtypeStruct((b, 512, t), BF16),
        jax.ShapeDtypeStruct((b, t, 512), BF16),
        jax.ShapeDtypeStruct((b, 4, LANES + ONES_ROWS, t), BF16),
        jax.ShapeDtypeStruct((b, t, 1536), F32),
        jax.ShapeDtypeStruct((b, t, 512), BF16),
        jax.ShapeDtypeStruct((b, t, LANES), F32),
    )
    out_specs = (
        tile(d),
        pl.BlockSpec((1, 512, TM), lambda i, j: (i, 0, j)),
        tile(LANES),
        pl.BlockSpec((1, 2, HEAD_DIM + ONES_ROWS, TM), lambda i, j: (i, 0, 0, j)),
        pl.BlockSpec((1, 512, TM), lambda i, j: (i, 0, j)),
        tile(512),
        pl.BlockSpec((1, 4, LANES + ONES_ROWS, TM), lambda i, j: (i, 0, 0, j)),
        tile(1536), tile(512), tile(LANES),
    )
    return pl.pallas_call(
        functools.partial(_inproj_kernel, nb=nb, ctx_tiles=ctx_tiles, d=d),
        grid=(b, nt),
        in_specs=[tile(d), whole(mod), whole(nw), whole(w_in_p),
                  pl.BlockSpec((TM, LANES), lambda i, j: (j, 0)),
                  pl.BlockSpec((TM, LANES), lambda i, j: (j, 0)),
                  whole(qkn), whole(bd)],
        out_specs=out_specs,
        out_shape=out_shape,
        compiler_params=_cparams(("parallel", "parallel")),
        name="inproj",
    )(xs, mod, nw, w_in_p, cos_t, sin_t, qkn, bd)


def _softmax_step(st, vt, mx, acc):
    mnew = jnp.maximum(mx, jnp.max(st, axis=0, keepdims=True))
    p = jnp.exp(st - mnew).astype(BF16)
    acc = acc * jnp.exp(mx - mnew) + jnp.dot(vt, p, preferred_element_type=F32)
    return mnew, acc


def _attend(w, k_at, vt_at, n_ctx, n_pairs, tk, sa_ref, sb_ref, vrows):
    def scores(off, n):
        return jnp.dot(k_at(off, n), w, preferred_element_type=F32)

    def lat(c):
        return pl.multiple_of(n_ctx + c * tk, TM)

    mx = jnp.full((1, w.shape[1]), -jnp.inf, F32)
    acc = jnp.zeros((vrows, w.shape[1]), F32)
    mx, acc = _softmax_step(scores(0, n_ctx), vt_at(0, n_ctx), mx, acc)

    def pair(i, carry, prefetch):
        mx, acc = carry
        sb_ref[...] = scores(lat(2 * i + 1), tk)
        mx, acc = _softmax_step(sa_ref[...], vt_at(lat(2 * i), tk), mx, acc)
        if prefetch:
            sa_ref[...] = scores(lat(2 * i + 2), tk)
        return _softmax_step(sb_ref[...], vt_at(lat(2 * i + 1), tk), mx, acc)

    def latent(carry):
        sa_ref[...] = scores(lat(0), tk)
        carry = lax.fori_loop(0, n_pairs - 1, functools.partial(pair, prefetch=True), carry)
        return pair(n_pairs - 1, carry, False)

    return lax.cond(pl.program_id(2) * TM < n_ctx, lambda carry: carry, latent, (mx, acc))[1]


def _gqa_kernel(qt_ref, k_ref, vt_ref, o_ref, sa_ref, sb_ref, *, n_ctx, n_pairs, tk):
    qt = qt_ref[0]
    top = jnp.concatenate([qt[:HEAD_DIM], qt[HEAD_DIM:]], axis=1)
    zero = jnp.zeros_like(top)
    w = jnp.where(pl.program_id(1) < 2, jnp.concatenate([top, zero], axis=0), jnp.concatenate([zero, top], axis=0))
    vrows = HEAD_DIM + ONES_ROWS
    acc = _attend(w, lambda off, n: k_ref[0, pl.ds(off, n), :], lambda off, n: vt_ref[0, 0, :, pl.ds(off, n)],
                  n_ctx, n_pairs, tk, sa_ref, sb_ref, vrows)
    o = acc[:HEAD_DIM] / acc[HEAD_DIM:HEAD_DIM + 1]
    o_ref[0] = jnp.concatenate([o[:, :TM], o[:, TM:]], axis=0).T.astype(BF16)


def _attn_chunk(n_lat):
    tk = min(512, n_lat // 2)
    assert n_lat % (2 * tk) == 0 and tk % TM == 0
    return tk, n_lat // (2 * tk)


def _gqa_attention(qat, ka, vat, *, n_ctx):
    b, _, t = qat.shape
    tk, n_pairs = _attn_chunk(t - n_ctx)
    vrows = vat.shape[2]
    return pl.pallas_call(
        functools.partial(_gqa_kernel, n_ctx=n_ctx, n_pairs=n_pairs, tk=tk),
        grid=(b, 4, t // TM),
        in_specs=[pl.BlockSpec((1, LANES, TM), lambda i, g, j: (i, g, j)),
                  pl.BlockSpec((1, t, LANES), lambda i, g, j: (i, 0, 0)),
                  pl.BlockSpec((1, 1, vrows, t), lambda i, g, j: (i, g // 2, 0, 0))],
        out_specs=pl.BlockSpec((1, TM, LANES), lambda i, g, j: (i, j, g)),
        out_shape=jax.ShapeDtypeStruct((b, t, 512), BF16),
        scratch_shapes=[pltpu.VMEM((tk, 2 * TM), F32), pltpu.VMEM((tk, 2 * TM), F32)],
        compiler_params=_cparams(("parallel", "parallel", "parallel")),
        name="gqa_attention",
    )(qat, ka, vat)


def _diff_kernel(qt_ref, k_ref, vt_ref, lam_ref, sw_ref, o_ref, sa_ref, sb_ref, *, n_ctx, n_pairs, tk, lam_init):
    qt = qt_ref[0]
    zero = jnp.zeros((HEAD_DIM, TM), BF16)
    w = jnp.concatenate([jnp.concatenate([qt[:HEAD_DIM], zero], axis=0),
                         jnp.concatenate([zero, qt[HEAD_DIM:]], axis=0)], axis=1)
    vrows = LANES + ONES_ROWS
    acc = _attend(w, lambda off, n: k_ref[0, pl.ds(off, n), :], lambda off, n: vt_ref[0, 0, :, pl.ds(off, n)],
                  n_ctx, n_pairs, tk, sa_ref, sb_ref, vrows)
    prod = lam_ref[0:1, :] * lam_ref[1:2, :]
    low = lax.broadcasted_iota(jnp.int32, prod.shape, 1) < HEAD_DIM
    l1 = jnp.sum(jnp.where(low, prod, 0.0), axis=-1, keepdims=True)
    l2 = jnp.sum(jnp.where(low, 0.0, prod), axis=-1, keepdims=True)
    lam = jnp.exp(l1) - jnp.exp(l2) + lam_init
    o1 = acc[:LANES, :TM] / acc[LANES:LANES + 1, :TM]
    o2 = acc[:LANES, TM:] / acc[LANES:LANES + 1, TM:]
    o = (o1 - lam * o2).T
    o_ref[0] = (_rms(o, sw_ref[...]) * (1.0 - lam_init)).astype(BF16)


def _diff_attention(qbt, kb, vbt, lam_rows, subln, *, n_ctx, lam_init):
    b, _, t = qbt.shape
    tk, n_pairs = _attn_chunk(t - n_ctx)
    vrows = vbt.shape[2]
    return pl.pallas_call(
        functools.partial(_diff_kernel, n_ctx=n_ctx, n_pairs=n_pairs, tk=tk, lam_init=lam_init),
        grid=(b, DIFF_HEADS, t // TM),
        in_specs=[pl.BlockSpec((1, LANES, TM), lambda i, g, j: (i, g, j)),
                  pl.BlockSpec((1, t, LANES), lambda i, g, j: (i, 0, g)),
                  pl.BlockSpec((1, 1, vrows, t), lambda i, g, j: (i, g, 0, 0)),
                  pl.BlockSpec(lam_rows.shape, lambda i, g, j: (0, 0)),
                  pl.BlockSpec(subln.shape, lambda i, g, j: (0, 0))],
        out_specs=pl.BlockSpec((1, TM, LANES), lambda i, g, j: (i, j, g)),
        out_shape=jax.ShapeDtypeStruct((b, t, 512), BF16),
        scratch_shapes=[pltpu.VMEM((tk, 2 * TM), F32), pltpu.VMEM((tk, 2 * TM), F32)],
        compiler_params=_cparams(("parallel", "parallel", "parallel")),
        name="diff_attention",
    )(qbt, kb, vbt, lam_rows, subln)


def _gdn_prep_kernel(c_ref, cp_ref, cn_ref, g_ref, cw_ref, gp_ref, q_ref, k_ref, v_ref, bg_ref, *, ctx_tiles, nt):
    t = pl.program_id(1)
    x = c_ref[0]
    first = jnp.logical_or(t == 0, t == ctx_tiles)
    last = jnp.logical_or(t == ctx_tiles - 1, t == nt - 1)
    prev_row = jnp.where(first, 0.0, cp_ref[0, 7:8, :])
    next_row = jnp.where(last, 0.0, cn_ref[0, 0:1, :])
    ridx = lax.broadcasted_iota(jnp.int32, x.shape, 0)
    xm = jnp.where(ridx == 0, prev_row, pltpu.roll(x, 1, axis=0))
    xp = jnp.where(ridx == TM - 1, next_row, pltpu.roll(x, TM - 1, axis=0))
    y = _silu(xm * cw_ref[0:1, :] + x * cw_ref[1:2, :] + xp * cw_ref[2:3, :])
    for h in range(GDN_HEADS):
        sl = slice(h * GDN_DK, (h + 1) * GDN_DK)
        qh = y[:, sl]
        q_ref[0, :, sl] = (qh * lax.rsqrt(jnp.sum(qh * qh, axis=-1, keepdims=True) + EPS)
                           * (GDN_DK ** -0.5)).astype(BF16)
        kh = y[:, 512 + h * GDN_DK:512 + (h + 1) * GDN_DK]
        k_ref[0, :, sl] = (kh * lax.rsqrt(jnp.sum(kh * kh, axis=-1, keepdims=True) + EPS)).astype(BF16)
    v_ref[0] = y[:, 1024:1536].astype(BF16)
    gx = g_ref[0]
    lane = lax.broadcasted_iota(jnp.int32, gx.shape, 1)
    a = gx + gp_ref[1:2, :]
    softplus = jnp.maximum(a, 0.0) + jnp.log(1.0 + jnp.exp(-jnp.abs(a)))
    gdec = -jnp.exp(gp_ref[0:1, :]) * softplus
    bg_ref[0] = jnp.where(lane < 8, _sigmoid(gx), jnp.where(lane < 16, gdec, 0.0))


def _gdn_prep(cqkv, gates, conv_w8, gparams, *, ctx_tiles):
    b, t, cch = cqkv.shape
    nt = t // TM
    r8 = TM // 8
    tile = lambda width: pl.BlockSpec((1, TM, width), lambda i, j: (i, j, 0))
    return pl.pallas_call(
        functools.partial(_gdn_prep_kernel, ctx_tiles=ctx_tiles, nt=nt),
        grid=(b, nt),
        in_specs=[tile(cch),
                  pl.BlockSpec((1, 8, cch), lambda i, j: (i, jnp.maximum(j * r8 - 1, 0), 0)),
                  pl.BlockSpec((1, 8, cch), lambda i, j: (i, jnp.minimum((j + 1) * r8, nt * r8 - 1), 0)),
                  tile(LANES),
                  pl.BlockSpec(conv_w8.shape, lambda i, j: (0, 0)),
                  pl.BlockSpec(gparams.shape, lambda i, j: (0, 0))],
        out_specs=(tile(512), tile(512), tile(512), tile(LANES)),
        out_shape=(jax.ShapeDtypeStruct((b, t, 512), BF16), jax.ShapeDtypeStruct((b, t, 512), BF16),
                   jax.ShapeDtypeStruct((b, t, 512), BF16), jax.ShapeDtypeStruct((b, t, LANES), F32)),
        compiler_params=_cparams(("parallel", "parallel")),
        name="gdn_prep",
    )(cqkv, cqkv, cqkv, gates, conv_w8, gparams)


def _unit_tri_inverses(mats):
    n = mats[0].shape[0]
    r = lax.broadcasted_iota(jnp.int32, (n, n), 0)
    c = lax.broadcasted_iota(jnp.int32, (n, n), 1)
    eye = jnp.where(r == c, 1.0, 0.0)
    ps = [-a for a in mats]
    ts = [eye + p for p in ps]
    for _ in range(5):
        ps = [_mm3(p, p) for p in ps]
        ts = [t + _mm3(t, p) for t, p in zip(ts, ps)]
    return ts


def _gdn_chunk_kernel(q_ref, k_ref, v_ref, bg_ref, l1_ref, l2_ref, u_ref, gl_ref):
    L = GDN_CHUNK
    r = lax.broadcasted_iota(jnp.int32, (L, L), 0)
    c = lax.broadcasted_iota(jnp.int32, (L, L), 1)
    bg = bg_ref[0]
    chains = [(d, h) for d in range(2) for h in range(GDN_HEADS)]
    incl = [(r >= c), (r <= c)]
    strict = [(r > c), (r < c)]
    gcs = [_mm_exact_lhs(jnp.where(m, 1.0, 0.0).astype(BF16), bg) for m in incl]
    gcts = [g.T for g in gcs]
    sl = [slice(h * GDN_DK, (h + 1) * GDN_DK) for h in range(GDN_HEADS)]
    qs = [q_ref[0, :, s] for s in sl]
    ks = [k_ref[0, :, s] for s in sl]
    kfs = [k.astype(F32) for k in ks]
    gcc = [gcs[d][:, 8 + 4 * d + h:9 + 4 * d + h] for d, h in chains]
    gcr = [gcts[d][8 + 4 * d + h:9 + 4 * d + h, :] for d, h in chains]
    dec = [jnp.where(incl[d], jnp.exp(jnp.where(incl[d], gcc[i] - gcr[i], 0.0)), 0.0) for i, (d, h) in enumerate(chains)]
    beta = [bg[:, 4 * d + h:4 * d + h + 1] for d, h in chains]
    kbeta = [kfs[h] * beta[i] for i, (d, h) in enumerate(chains)]
    amat = [jnp.where(strict[d], _mm_nt(kbeta[i], ks[h]) * dec[i], 0.0) for i, (d, h) in enumerate(chains)]
    tinv = _unit_tri_inverses(amat)
    eg = [jnp.exp(g) for g in gcc]
    rhs = [jnp.concatenate([v_ref[0, :, sl[h]].astype(F32) * beta[i], kbeta[i] * eg[i]], axis=1)
           for i, (d, h) in enumerate(chains)]
    sol = [_mm3(t, x) for t, x in zip(tinv, rhs)]
    qk = [_mm_nt(qs[h], ks[h]) for h in range(GDN_HEADS)]
    for i, (d, h) in enumerate(chains):
        last = L - 1 if d == 0 else 0
        glast = gcc[i][last:last + 1, :]
        kdec = kfs[h] * jnp.exp(glast - gcc[i])
        u_ref[0, d, 0, h] = sol[i][:, :GDN_DK]
        l1_ref[0, d, 0, h, 0:L, :] = sol[i][:, GDN_DK:].astype(BF16)
        l1_ref[0, d, 0, h, L:2 * L, :] = (qs[h].astype(F32) * eg[i]).astype(BF16)
        l2_ref[0, d, 0, h, 0:L, :] = (qk[h] * dec[i]).astype(BF16)
        l2_ref[0, d, 0, h, L:, :] = kdec.T.astype(BF16)
        gl_ref[0, d, 0, h:h + 1, :] = jnp.broadcast_to(jnp.exp(glast), (1, LANES))
    for d in range(2):
        gl_ref[0, d, 0, GDN_HEADS:, :] = jnp.zeros((8 - GDN_HEADS, LANES), F32)


def _gdn_chunks(qg, kg, vg, bg):
    b, t, _ = qg.shape
    nc = t // GDN_CHUNK
    L = GDN_CHUNK
    tile = lambda width: pl.BlockSpec((1, L, width), lambda i, j: (i, j, 0))
    return pl.pallas_call(
        _gdn_chunk_kernel,
        grid=(b, nc),
        in_specs=[tile(512), tile(512), tile(512), tile(LANES)],
        out_specs=(pl.BlockSpec((1, 2, 1, 4, 2 * L, GDN_DK), lambda i, j: (i, 0, j, 0, 0, 0)),
                   pl.BlockSpec((1, 2, 1, 4, L + GDN_DK, L), lambda i, j: (i, 0, j, 0, 0, 0)),
                   pl.BlockSpec((1, 2, 1, 4, L, GDN_DK), lambda i, j: (i, 0, j, 0, 0, 0)),
                   pl.BlockSpec((1, 2, 1, 8, LANES), lambda i, j: (i, 0, j, 0, 0))),
        out_shape=(jax.ShapeDtypeStruct((b, 2, nc, 4, 2 * L, GDN_DK), BF16),
                   jax.ShapeDtypeStruct((b, 2, nc, 4, L + GDN_DK, L), BF16),
                   jax.ShapeDtypeStruct((b, 2, nc, 4, L, GDN_DK), F32),
                   jax.ShapeDtypeStruct((b, 2, nc, 8, LANES), F32)),
        compiler_params=_cparams(("parallel", "parallel")),
        name="gdn_chunks",
    )(qg, kg, vg, bg)


def _gdn_scan_kernel(l1f_ref, l2f_ref, uf_ref, glf_ref, l1b_ref, l2b_ref, ub_ref, glb_ref,
                     of_ref, ob_ref, s_ref):
    L = GDN_CHUNK

    @pl.when(pl.program_id(1) == 0)
    def _():
        s_ref[...] = jnp.zeros_like(s_ref)

    dirs = ((l1f_ref, l2f_ref, uf_ref, glf_ref, of_ref), (l1b_ref, l2b_ref, ub_ref, glb_ref, ob_ref))
    chains = [(d, h) for d in range(2) for h in range(GDN_HEADS)]
    ss = [s_ref[i] for i in range(len(chains))]
    r1 = [_mm(dirs[d][0][0, 0, 0, h], ss[i]) for i, (d, h) in enumerate(chains)]
    v_new = [dirs[d][2][0, 0, 0, h] - r1[i][:L] for i, (d, h) in enumerate(chains)]
    r2 = [_mm(dirs[d][1][0, 0, 0, h], v_new[i]) for i, (d, h) in enumerate(chains)]
    for i, (d, h) in enumerate(chains):
        dirs[d][4][0, :, h * GDN_DK:(h + 1) * GDN_DK] = r1[i][L:] + r2[i][:L]
        s_ref[i] = ss[i] * dirs[d][3][0, 0, 0, h:h + 1, :] + r2[i][L:]


def _gdn_scan(l1, l2, u, gl, *, ctx_chunks):
    b, _, nc = l1.shape[:3]
    L = GDN_CHUNK

    def rev(j):
        return jnp.where(j < ctx_chunks, ctx_chunks - 1 - j, nc - 1 - (j - ctx_chunks))

    def spec(a, d):
        blk = (1, 1, 1) + a.shape[3:]
        zeros = (0,) * (a.ndim - 3)
        if d == 0:
            return pl.BlockSpec(blk, lambda i, j: (i, 0, j) + zeros)
        return pl.BlockSpec(blk, lambda i, j: (i, 1, rev(j)) + zeros)

    return pl.pallas_call(
        _gdn_scan_kernel,
        grid=(b, nc),
        in_specs=[spec(l1, 0), spec(l2, 0), spec(u, 0), spec(gl, 0),
                  spec(l1, 1), spec(l2, 1), spec(u, 1), spec(gl, 1)],
        out_specs=(pl.BlockSpec((1, L, 512), lambda i, j: (i, j, 0)),
                   pl.BlockSpec((1, L, 512), lambda i, j: (i, rev(j), 0))),
        out_shape=(jax.ShapeDtypeStruct((b, nc * L, 512), F32), jax.ShapeDtypeStruct((b, nc * L, 512), F32)),
        scratch_shapes=[pltpu.VMEM((2 * GDN_HEADS, GDN_DK, GDN_DK), F32)],
        compiler_params=_cparams(("parallel", "arbitrary")),
        name="gdn_scan",
    )(l1, l2, u, gl, l1, l2, u, gl)


def _merge_kernel(x_ref, h_ref, ya_ref, yb_ref, of_ref, ob_ref, z_ref, mod_ref, gnw_ref, n2w_ref,
                  wg_ref, wb_ref, wo_ref, wr_ref, xo_ref, h2_ref, aff_ref, *, nb, ctx_tiles, d):
    m = _mod_row(mod_ref, nb, ctx_tiles)
    g1, sh2, sc2 = m[:, 2 * d:3 * d], m[:, 3 * d:4 * d], m[:, 4 * d:5 * d]
    o = of_ref[0] + ob_ref[0]
    z = z_ref[0].astype(F32)
    ygs = []
    for h in range(GDN_HEADS):
        sl = slice(h * GDN_DK, (h + 1) * GDN_DK)
        ygs.append(_rms(o[:, sl], gnw_ref[...]) * _silu(z[:, sl]))
    yg = jnp.concatenate(ygs, axis=1).astype(BF16)
    hb = h_ref[0]
    merged = jnp.zeros((TM, d), F32)
    for i, y in enumerate((ya_ref[0], yb_ref[0], yg)):
        gate = _sigmoid(jnp.dot(hb, wg_ref[i], preferred_element_type=F32))
        merged = merged + gate * jnp.dot(y, wb_ref[i], preferred_element_type=F32)
    x = x_ref[0] + g1 * jnp.dot(merged.astype(BF16), wo_ref[...], preferred_element_type=F32)
    xo_ref[0] = x
    h2 = _rms(x, n2w_ref[...]) * (1.0 + sc2) + sh2
    h2b = h2.astype(BF16)
    h2_ref[0] = h2b
    logits = _mm3(h2, wr_ref[...])
    lane = lax.broadcasted_iota(jnp.int32, logits.shape, 1)
    logits = jnp.where(lane < N_EXPERTS, logits, -jnp.inf)
    e = jnp.exp(logits - jnp.max(logits, axis=-1, keepdims=True))
    aff_ref[0] = e / jnp.sum(e, axis=-1, keepdims=True)


def _merge(xs, h, ya, yb, o_f, o_b, z, mod, gnw, n2w, wg, wb, wo, wr, *, nb, ctx_tiles):
    b, t, d = xs.shape
    nt = t // TM
    tile = lambda width: pl.BlockSpec((1, TM, width), lambda i, j: (i, j, 0))
    whole = lambda a: pl.BlockSpec(a.shape, lambda i, j: (0,) * a.ndim)
    return pl.pallas_call(
        functools.partial(_merge_kernel, nb=nb, ctx_tiles=ctx_tiles, d=d),
        grid=(b, nt),
        in_specs=[tile(d), tile(d), tile(512), tile(512), tile(512), tile(512), tile(512),
                  whole(mod), whole(gnw), whole(n2w), whole(wg), whole(wb), whole(wo), whole(wr)],
        out_specs=(tile(d), tile(d), tile(LANES)),
        out_shape=(jax.ShapeDtypeStruct((b, t, d), F32), jax.ShapeDtypeStruct((b, t, d), BF16),
                   jax.ShapeDtypeStruct((b, t, LANES), F32)),
        compiler_params=_cparams(("parallel", "parallel")),
        name="merge",
    )(xs, h, ya, yb, o_f, o_b, z, mod, gnw, n2w, wg, wb, wo, wr)


def _expert_kernel(x_ref, tw_ref, wg_ref, wu_ref, wd_ref, o_ref):
    x = x_ref[0, 0]
    hid = _silu(jnp.dot(x, wg_ref[0], preferred_element_type=F32)) * jnp.dot(x, wu_ref[0], preferred_element_type=F32)
    y = jnp.dot(hid.astype(BF16), wd_ref[0], preferred_element_type=F32)
    o_ref[0, 0] = y * tw_ref[0, 0]


def _experts(xin, tw, wg, wu, wd):
    b, e, cap, d = xin.shape
    ff = wg.shape[-1]
    return pl.pallas_call(
        _expert_kernel,
        grid=(e, b),
        in_specs=[pl.BlockSpec((1, 1, cap, d), lambda k, i: (i, k, 0, 0)),
                  pl.BlockSpec((1, 1, cap, 1), lambda k, i: (i, k, 0, 0)),
                  pl.BlockSpec((1, d, ff), lambda k, i: (k, 0, 0)),
                  pl.BlockSpec((1, d, ff), lambda k, i: (k, 0, 0)),
                  pl.BlockSpec((1, ff, d), lambda k, i: (k, 0, 0))],
        out_specs=pl.BlockSpec((1, 1, cap, d), lambda k, i: (i, k, 0, 0)),
        out_shape=jax.ShapeDtypeStruct((b, e, cap, d), F32),
        compiler_params=_cparams(("parallel", "parallel")),
        name="experts",
    )(xin, tw, wg, wu, wd)


def _moe_segment(h2, aff, wg, wu, wd):
    b, n, d = h2.shape
    cap = CAPACITY_FACTOR * n // N_EXPERTS
    top_w, top_i = lax.top_k(jnp.swapaxes(aff, 1, 2), cap)
    xin = jax.vmap(lambda hb, ib: hb[ib])(h2, top_i)
    y = _experts(xin, top_w[..., None], wg, wu, wd)

    def scatter(ib, yb):
        return jnp.zeros((n, d), F32).at[ib.reshape(-1)].add(yb.reshape(-1, d))

    return jax.vmap(scatter)(top_i, y)


def _lambda_init(layer_idx):
    return 0.8 - 0.6 * math.exp(-0.3 * layer_idx)


def _rope_tables(n_lat, n_ctx):
    rows = n_lat // GRID_W
    row = jnp.repeat(jnp.arange(rows, dtype=F32), GRID_W)
    col = jnp.tile(jnp.arange(GRID_W, dtype=F32), rows)
    axis_dim = HEAD_DIM // 2
    inv_freq = ROPE_THETA ** (-jnp.arange(0, axis_dim, 2, dtype=F32) / axis_dim)
    ang_r, ang_c = row[:, None] * inv_freq, col[:, None] * inv_freq
    cos = jnp.concatenate([jnp.cos(ang_r)] * 2 + [jnp.cos(ang_c)] * 2, axis=-1)
    sin = jnp.concatenate([-jnp.sin(ang_r), jnp.sin(ang_r), -jnp.sin(ang_c), jnp.sin(ang_c)], axis=-1)
    cos = jnp.concatenate([jnp.ones((n_ctx, HEAD_DIM), F32), cos], axis=0)
    sin = jnp.concatenate([jnp.zeros((n_ctx, HEAD_DIM), F32), sin], axis=0)
    return jnp.tile(cos, (1, 2)), jnp.tile(sin, (1, 2))


def kernel(x, c, ctx, c_ctx, w_mod, b_mod, norm1_w, norm2_w, w_in, gqa_q_norm, gqa_k_norm, diff_q_norm,
           diff_k_norm, diff_lambda_q1, diff_lambda_k1, diff_lambda_q2, diff_lambda_k2, diff_subln,
           gdn_conv_w, gdn_a_log, gdn_dt_bias, gdn_norm_w, w_merge_gate, w_branch, w_out, w_router,
           w_exp_gate, w_exp_up, w_exp_down):
    nb, n_lat, d = x.shape
    n_ctx = ctx.shape[1]
    depth = w_mod.shape[0]
    assert n_ctx % TM == 0 and n_lat % TM == 0 and nb < MOD_ROWS and d == 1024
    ctx_tiles = n_ctx // TM
    ctx_chunks = n_ctx // GDN_CHUNK

    cos_t, sin_t = _rope_tables(n_lat, n_ctx)
    cc = jnp.zeros((MOD_ROWS, d), F32).at[:nb].set(c).at[nb].set(c_ctx)
    mods = _modulation(cc, w_mod, b_mod)
    half = lax.broadcasted_iota(jnp.int32, (LANES, LANES), 0) // HEAD_DIM
    bd = (half == half.T).astype(BF16)

    xs = jnp.concatenate([ctx, x], axis=1)
    scale = HEAD_DIM ** -0.5
    for li in range(depth):
        lam_init = _lambda_init(li)
        w_in_p = jnp.pad(w_in[li], ((0, 0), (0, 4480 - w_in.shape[-1]))).astype(BF16)
        qkn = jnp.stack([jnp.tile(gqa_q_norm[li] * scale, 2), jnp.tile(gqa_k_norm[li], 2),
                         jnp.tile(diff_q_norm[li] * scale, 2), jnp.tile(diff_k_norm[li], 2)])
        (h, qa, ka, va, qb, kb, vb, cqkv, z, gates) = _inproj(
            xs, mods[li], norm1_w[li][None, :], w_in_p, cos_t, sin_t, qkn, bd, nb=nb, ctx_tiles=ctx_tiles)

        ya = _gqa_attention(qa, ka, va, n_ctx=n_ctx)
        lam_rows = jnp.stack([jnp.concatenate([diff_lambda_q1[li], diff_lambda_q2[li]]),
                              jnp.concatenate([diff_lambda_k1[li], diff_lambda_k2[li]])])
        yb = _diff_attention(qb, kb, vb, lam_rows, diff_subln[li][None, :], n_ctx=n_ctx, lam_init=lam_init)

        conv_w8 = jnp.pad(gdn_conv_w[li], ((0, 5), (0, 0)))
        gparams = jnp.zeros((2, LANES), F32)
        gparams = gparams.at[0, 8:16].set(gdn_a_log[li].reshape(-1)).at[1, 8:16].set(gdn_dt_bias[li].reshape(-1))
        qg, kg, vg, bg = _gdn_prep(cqkv, gates, conv_w8, gparams, ctx_tiles=ctx_tiles)
        l1, l2, u, gl = _gdn_chunks(qg, kg, vg, bg)
        o_f, o_b = _gdn_scan(l1, l2, u, gl, ctx_chunks=ctx_chunks)

        wr = jnp.pad(w_router[li], ((0, 0), (0, LANES - N_EXPERTS)))
        xs, h2, aff = _merge(xs, h, ya, yb, o_f, o_b, z, mods[li], gdn_norm_w[li][None, :],
                             norm2_w[li][None, :], w_merge_gate[li].astype(BF16), w_branch[li].astype(BF16),
                             w_out[li].astype(BF16), wr, nb=nb, ctx_tiles=ctx_tiles)

        wg, wu, wd = (w_exp_gate[li].astype(BF16), w_exp_up[li].astype(BF16), w_exp_down[li].astype(BF16))
        aff = aff[:, :, :N_EXPERTS]
        moe = jnp.concatenate([_moe_segment(h2[:, :n_ctx], aff[:, :n_ctx], wg, wu, wd),
                               _moe_segment(h2[:, n_ctx:], aff[:, n_ctx:], wg, wu, wd)], axis=1)
        g2 = jnp.concatenate([jnp.broadcast_to(mods[li, nb, 5 * d:][None, None, :], (nb, n_ctx, d)),
                              jnp.broadcast_to(mods[li, :nb, None, 5 * d:], (nb, n_lat, d))], axis=1)
        xs = xs + g2 * moe
    return xs[:, n_ctx:]
```

```python
import functools
import math

import jax
import jax.numpy as jnp
from jax import lax
from jax.experimental import pallas as pl
from jax.experimental.pallas import tpu as pltpu

F32 = jnp.float32
BF16 = jnp.bfloat16

LANES = 128
TM = 256
GRID_W = 64
HEAD_DIM = 64
GQA_HEADS = 8
GQA_KV_HEADS = 2
DIFF_HEADS = 4
GDN_HEADS = 4
GDN_DK = 128
GDN_CHUNK = 64
N_EXPERTS = 16
CAPACITY_FACTOR = 2
ROPE_THETA = 10000.0
EPS = 1e-6
ONES_ROWS = 16
MOD_ROWS = 16
VMEM_LIMIT = 56 * 1024 * 1024
MOE_FFN_VMEM_LIMIT = 60 * 1024 * 1024


def _cparams(sem):
    return pltpu.CompilerParams(dimension_semantics=sem, vmem_limit_bytes=VMEM_LIMIT)


def _mm(a, b):
    return jnp.dot(a.astype(BF16), b.astype(BF16), preferred_element_type=F32)


def _mm_nt(a, b):
    return lax.dot_general(a.astype(BF16), b.astype(BF16), (((1,), (1,)), ((), ())),
                           preferred_element_type=F32)


def _split2(a):
    hi = a.astype(BF16)
    lo = (a - hi.astype(F32)).astype(BF16)
    return hi, lo


def _mm3(a, b):
    ah, al = _split2(a)
    bh, bl = _split2(b)
    return _mm(ah, bh) + _mm(ah, bl) + _mm(al, bh)


def _mm_exact_lhs(a_bf16, b):
    b1 = b.astype(BF16)
    r1 = b - b1.astype(F32)
    b2 = r1.astype(BF16)
    b3 = (r1 - b2.astype(F32)).astype(BF16)
    return _mm(a_bf16, b1) + _mm(a_bf16, b2) + _mm(a_bf16, b3)


def _sigmoid(x):
    return 1.0 / (1.0 + jnp.exp(-x))


def _silu(x):
    return x * _sigmoid(x)


def _mod_kernel(a_ref, w_ref, b_ref, o_ref):
    a = _silu(a_ref[...])
    o_ref[0] = _mm3(a, w_ref[0]) + b_ref[0]


def _modulation(cc, w_mod, b_mod):
    depth, d, d6 = w_mod.shape
    nj = d6 // d
    return pl.pallas_call(
        _mod_kernel,
        grid=(depth, nj),
        in_specs=[pl.BlockSpec((MOD_ROWS, d), lambda l, j: (0, 0)),
                  pl.BlockSpec((1, d, d), lambda l, j: (l, 0, j)),
                  pl.BlockSpec((1, 1, d), lambda l, j: (l, 0, j))],
        out_specs=pl.BlockSpec((1, MOD_ROWS, d), lambda l, j: (l, 0, j)),
        out_shape=jax.ShapeDtypeStruct((depth, MOD_ROWS, d6), F32),
        compiler_params=_cparams(("parallel", "parallel")),
        name="modulation",
    )(cc, w_mod, b_mod.reshape(depth, 1, d6))


def _mod_row(mod_ref, nb, ctx_tiles):
    b = pl.program_id(0)
    t = pl.program_id(1)
    row = jnp.where(t < ctx_tiles, nb, b)
    return mod_ref[pl.ds(row, 1), :]


def _rms(x, w):
    return x * lax.rsqrt(jnp.mean(x * x, axis=-1, keepdims=True) + EPS) * w


def _half_norm_rope(t, w, bd, cos, sin):
    ssq = _mm(t * t, bd)
    tn = t * lax.rsqrt(ssq * (1.0 / HEAD_DIM) + EPS) * w
    lane = lax.broadcasted_iota(jnp.int32, tn.shape, 1)
    partner = jnp.where((lane % 32) < 16, pltpu.roll(tn, LANES - 16, axis=1), pltpu.roll(tn, 16, axis=1))
    return tn * cos + partner * sin


def _inproj_kernel(x_ref, mod_ref, nw_ref, w_ref, cos_ref, sin_ref, qkn_ref, bd_ref,
                   h_ref, qa_ref, ka_ref, va_ref, qb_ref, kb_ref, vb_ref, c_ref, z_ref, g_ref,
                   *, nb, ctx_tiles, d):
    m = _mod_row(mod_ref, nb, ctx_tiles)
    sh, sc = m[:, 0:d], m[:, d:2 * d]
    h = _rms(x_ref[0], nw_ref[...]) * (1.0 + sc) + sh
    hb = h.astype(BF16)
    h_ref[0] = hb
    cos, sin, bd = cos_ref[...], sin_ref[...], bd_ref[...]

    def proj(lo, width):
        return jnp.dot(hb, w_ref[:, lo:lo + width], preferred_element_type=F32)

    ones_rows = jnp.ones((ONES_ROWS, TM), BF16)
    for g in range(4):
        t = _half_norm_rope(proj(g * LANES, LANES), qkn_ref[0:1, :], bd, cos, sin)
        qa_ref[0, g * LANES:(g + 1) * LANES, :] = t.T.astype(BF16)
    ka_ref[0] = _half_norm_rope(proj(512, LANES), qkn_ref[1:2, :], bd, cos, sin).astype(BF16)
    vt = proj(640, LANES).T.astype(BF16)
    for j in range(GQA_KV_HEADS):
        va_ref[0, j, 0:HEAD_DIM, :] = vt[j * HEAD_DIM:(j + 1) * HEAD_DIM]
        va_ref[0, j, HEAD_DIM:HEAD_DIM + ONES_ROWS, :] = ones_rows
    for g in range(4):
        t = _half_norm_rope(proj(768 + g * LANES, LANES), qkn_ref[2:3, :], bd, cos, sin)
        qb_ref[0, g * LANES:(g + 1) * LANES, :] = t.T.astype(BF16)
        t = proj(1280 + g * LANES, LANES)
        kb_ref[0, :, g * LANES:(g + 1) * LANES] = _half_norm_rope(t, qkn_ref[3:4, :], bd, cos, sin).astype(BF16)
        vb_ref[0, g, 0:LANES, :] = proj(1792 + g * LANES, LANES).T.astype(BF16)
        vb_ref[0, g, LANES:LANES + ONES_ROWS, :] = ones_rows
    for g in range(3):
        c_ref[0, :, g * 512:(g + 1) * 512] = proj(2304 + g * 512, 512)
    z_ref[0] = proj(3840, 512).astype(BF16)
    g_ref[0] = proj(4352, LANES)


def _inproj(xs, mod, nw, w_in_p, cos_t, sin_t, qkn, bd, *, nb, ctx_tiles):
    b, t, d = xs.shape
    nt = t // TM
    tile = lambda width: pl.BlockSpec((1, TM, width), lambda i, j: (i, j, 0))
    whole = lambda a: pl.BlockSpec(a.shape, lambda i, j: (0,) * a.ndim)
    out_shape = (
        jax.ShapeDtypeStruct((b, t, d), BF16),
        jax.ShapeDtypeStruct((b, 512, t), BF16),
        jax.ShapeDtypeStruct((b, t, LANES), BF16),
        jax.ShapeDtypeStruct((b, 2, HEAD_DIM + ONES_ROWS, t), BF16),
        jax.ShapeDtypeStruct((b, 512, t), BF16),
        jax.ShapeDtypeStruct((b, t, 512), BF16),
        jax.ShapeDtypeStruct((b, 4, LANES + ONES_ROWS, t), BF16),
        jax.ShapeDtypeStruct((b, t, 1536), F32),
        jax.ShapeDtypeStruct((b, t, 512), BF16),
        jax.ShapeDtypeStruct((b, t, LANES), F32),
    )
    out_specs = (
        tile(d),
        pl.BlockSpec((1, 512, TM), lambda i, j: (i, 0, j)),
        tile(LANES),
        pl.BlockSpec((1, 2, HEAD_DIM + ONES_ROWS, TM), lambda i, j: (i, 0, 0, j)),
        pl.BlockSpec((1, 512, TM), lambda i, j: (i, 0, j)),
        tile(512),
        pl.BlockSpec((1, 4, LANES + ONES_ROWS, TM), lambda i, j: (i, 0, 0, j)),
        tile(1536), tile(512), tile(LANES),
    )
    return pl.pallas_call(
        functools.partial(_inproj_kernel, nb=nb, ctx_tiles=ctx_tiles, d=d),
        grid=(b, nt),
        in_specs=[tile(d), whole(mod), whole(nw), whole(w_in_p),
                  pl.BlockSpec((TM, LANES), lambda i, j: (j, 0)),
                  pl.BlockSpec((TM, LANES), lambda i, j: (j, 0)),
                  whole(qkn), whole(bd)],
        out_specs=out_specs,
        out_shape=out_shape,
        compiler_params=_cparams(("parallel", "parallel")),
        name="inproj",
    )(xs, mod, nw, w_in_p, cos_t, sin_t, qkn, bd)


def _softmax_step(st, vt, mx, acc):
    mnew = jnp.maximum(mx, jnp.max(st, axis=0, keepdims=True))
    p = jnp.exp(st - mnew).astype(BF16)
    acc = acc * jnp.exp(mx - mnew) + jnp.dot(vt, p, preferred_element_type=F32)
    return mnew, acc


def _attend(w, k_at, vt_at, n_ctx, n_pairs, tk, sa_ref, sb_ref, vrows):
    def scores(off, n):
        return jnp.dot(k_at(off, n), w, preferred_element_type=F32)

    def lat(c):
        return pl.multiple_of(n_ctx + c * tk, TM)

    mx = jnp.full((1, w.shape[1]), -jnp.inf, F32)
    acc = jnp.zeros((vrows, w.shape[1]), F32)
    mx, acc = _softmax_step(scores(0, n_ctx), vt_at(0, n_ctx), mx, acc)

    def pair(i, carry, prefetch):
        mx, acc = carry
        sb_ref[...] = scores(lat(2 * i + 1), tk)
        mx, acc = _softmax_step(sa_ref[...], vt_at(lat(2 * i), tk), mx, acc)
        if prefetch:
            sa_ref[...] = scores(lat(2 * i + 2), tk)
        return _softmax_step(sb_ref[...], vt_at(lat(2 * i + 1), tk), mx, acc)

    def latent(carry):
        sa_ref[...] = scores(lat(0), tk)
        carry = lax.fori_loop(0, n_pairs - 1, functools.partial(pair, prefetch=True), carry)
        return pair(n_pairs - 1, carry, False)

    return lax.cond(pl.program_id(2) * TM < n_ctx, lambda carry: carry, latent, (mx, acc))[1]


def _gqa_kernel(qt_ref, k_ref, vt_ref, o_ref, sa_ref, sb_ref, *, n_ctx, n_pairs, tk):
    qt = qt_ref[0]
    top = jnp.concatenate([qt[:HEAD_DIM], qt[HEAD_DIM:]], axis=1)
    zero = jnp.zeros_like(top)
    w = jnp.where(pl.program_id(1) < 2, jnp.concatenate([top, zero], axis=0), jnp.concatenate([zero, top], axis=0))
    vrows = HEAD_DIM + ONES_ROWS
    acc = _attend(w, lambda off, n: k_ref[0, pl.ds(off, n), :], lambda off, n: vt_ref[0, 0, :, pl.ds(off, n)],
                  n_ctx, n_pairs, tk, sa_ref, sb_ref, vrows)
    o = acc[:HEAD_DIM] / acc[HEAD_DIM:HEAD_DIM + 1]
    o_ref[0] = jnp.concatenate([o[:, :TM], o[:, TM:]], axis=0).T.astype(BF16)


def _attn_chunk(n_lat):
    tk = min(512, n_lat // 2)
    assert n_lat % (2 * tk) == 0 and tk % TM == 0
    return tk, n_lat // (2 * tk)


def _gqa_attention(qat, ka, vat, *, n_ctx):
    b, _, t = qat.shape
    tk, n_pairs = _attn_chunk(t - n_ctx)
    vrows = vat.shape[2]
    return pl.pallas_call(
        functools.partial(_gqa_kernel, n_ctx=n_ctx, n_pairs=n_pairs, tk=tk),
        grid=(b, 4, t // TM),
        in_specs=[pl.BlockSpec((1, LANES, TM), lambda i, g, j: (i, g, j)),
                  pl.BlockSpec((1, t, LANES), lambda i, g, j: (i, 0, 0)),
                  pl.BlockSpec((1, 1, vrows, t), lambda i, g, j: (i, g // 2, 0, 0))],
        out_specs=pl.BlockSpec((1, TM, LANES), lambda i, g, j: (i, j, g)),
        out_shape=jax.ShapeDtypeStruct((b, t, 512), BF16),
        scratch_shapes=[pltpu.VMEM((tk, 2 * TM), F32), pltpu.VMEM((tk, 2 * TM), F32)],
        compiler_params=_cparams(("parallel", "parallel", "parallel")),
        name="gqa_attention",
    )(qat, ka, vat)


def _diff_kernel(qt_ref, k_ref, vt_ref, lam_ref, sw_ref, o_ref, sa_ref, sb_ref, *, n_ctx, n_pairs, tk, lam_init):
    qt = qt_ref[0]
    zero = jnp.zeros((HEAD_DIM, TM), BF16)
    w = jnp.concatenate([jnp.concatenate([qt[:HEAD_DIM], zero], axis=0),
                         jnp.concatenate([zero, qt[HEAD_DIM:]], axis=0)], axis=1)
    vrows = LANES + ONES_ROWS
    acc = _attend(w, lambda off, n: k_ref[0, pl.ds(off, n), :], lambda off, n: vt_ref[0, 0, :, pl.ds(off, n)],
                  n_ctx, n_pairs, tk, sa_ref, sb_ref, vrows)
    prod = lam_ref[0:1, :] * lam_ref[1:2, :]
    low = lax.broadcasted_iota(jnp.int32, prod.shape, 1) < HEAD_DIM
    l1 = jnp.sum(jnp.where(low, prod, 0.0), axis=-1, keepdims=True)
    l2 = jnp.sum(jnp.where(low, 0.0, prod), axis=-1, keepdims=True)
    lam = jnp.exp(l1) - jnp.exp(l2) + lam_init
    o1 = acc[:LANES, :TM] / acc[LANES:LANES + 1, :TM]
    o2 = acc[:LANES, TM:] / acc[LANES:LANES + 1, TM:]
    o = (o1 - lam * o2).T
    o_ref[0] = (_rms(o, sw_ref[...]) * (1.0 - lam_init)).astype(BF16)


def _diff_attention(qbt, kb, vbt, lam_rows, subln, *, n_ctx, lam_init):
    b, _, t = qbt.shape
    tk, n_pairs = _attn_chunk(t - n_ctx)
    vrows = vbt.shape[2]
    return pl.pallas_call(
        functools.partial(_diff_kernel, n_ctx=n_ctx, n_pairs=n_pairs, tk=tk, lam_init=lam_init),
        grid=(b, DIFF_HEADS, t // TM),
        in_specs=[pl.BlockSpec((1, LANES, TM), lambda i, g, j: (i, g, j)),
                  pl.BlockSpec((1, t, LANES), lambda i, g, j: (i, 0, g)),
                  pl.BlockSpec((1, 1, vrows, t), lambda i, g, j: (i, g, 0, 0)),
                  pl.BlockSpec(lam_rows.shape, lambda i, g, j: (0, 0)),
                  pl.BlockSpec(subln.shape, lambda i, g, j: (0, 0))],
        out_specs=pl.BlockSpec((1, TM, LANES), lambda i, g, j: (i, j, g)),
        out_shape=jax.ShapeDtypeStruct((b, t, 512), BF16),
        scratch_shapes=[pltpu.VMEM((tk, 2 * TM), F32), pltpu.VMEM((tk, 2 * TM), F32)],
        compiler_params=_cparams(("parallel", "parallel", "parallel")),
        name="diff_attention",
    )(qbt, kb, vbt, lam_rows, subln)


def _gdn_prep_kernel(c_ref, cp_ref, cn_ref, g_ref, cw_ref, gp_ref, q_ref, k_ref, v_ref, bg_ref, *, ctx_tiles, nt):
    t = pl.program_id(1)
    x = c_ref[0]
    first = jnp.logical_or(t == 0, t == ctx_tiles)
    last = jnp.logical_or(t == ctx_tiles - 1, t == nt - 1)
    prev_row = jnp.where(first, 0.0, cp_ref[0, 7:8, :])
    next_row = jnp.where(last, 0.0, cn_ref[0, 0:1, :])
    ridx = lax.broadcasted_iota(jnp.int32, x.shape, 0)
    xm = jnp.where(ridx == 0, prev_row, pltpu.roll(x, 1, axis=0))
    xp = jnp.where(ridx == TM - 1, next_row, pltpu.roll(x, TM - 1, axis=0))
    y = _silu(xm * cw_ref[0:1, :] + x * cw_ref[1:2, :] + xp * cw_ref[2:3, :])
    for h in range(GDN_HEADS):
        sl = slice(h * GDN_DK, (h + 1) * GDN_DK)
        qh = y[:, sl]
        q_ref[0, :, sl] = (qh * lax.rsqrt(jnp.sum(qh * qh, axis=-1, keepdims=True) + EPS)
                           * (GDN_DK ** -0.5)).astype(BF16)
        kh = y[:, 512 + h * GDN_DK:512 + (h + 1) * GDN_DK]
        k_ref[0, :, sl] = (kh * lax.rsqrt(jnp.sum(kh * kh, axis=-1, keepdims=True) + EPS)).astype(BF16)
    v_ref[0] = y[:, 1024:1536].astype(BF16)
    gx = g_ref[0]
    lane = lax.broadcasted_iota(jnp.int32, gx.shape, 1)
    a = gx + gp_ref[1:2, :]
    softplus = jnp.maximum(a, 0.0) + jnp.log(1.0 + jnp.exp(-jnp.abs(a)))
    gdec = -jnp.exp(gp_ref[0:1, :]) * softplus
    bg_ref[0] = jnp.where(lane < 8, _sigmoid(gx), jnp.where(lane < 16, gdec, 0.0))


def _gdn_prep(cqkv, gates, conv_w8, gparams, *, ctx_tiles):
    b, t, cch = cqkv.shape
    nt = t // TM
    r8 = TM // 8
    tile = lambda width: pl.BlockSpec((1, TM, width), lambda i, j: (i, j, 0))
    return pl.pallas_call(
        functools.partial(_gdn_prep_kernel, ctx_tiles=ctx_tiles, nt=nt),
        grid=(b, nt),
        in_specs=[tile(cch),
                  pl.BlockSpec((1, 8, cch), lambda i, j: (i, jnp.maximum(j * r8 - 1, 0), 0)),
                  pl.BlockSpec((1, 8, cch), lambda i, j: (i, jnp.minimum((j + 1) * r8, nt * r8 - 1), 0)),
                  tile(LANES),
                  pl.BlockSpec(conv_w8.shape, lambda i, j: (0, 0)),
                  pl.BlockSpec(gparams.shape, lambda i, j: (0, 0))],
        out_specs=(tile(512), tile(512), tile(512), tile(LANES)),
        out_shape=(jax.ShapeDtypeStruct((b, t, 512), BF16), jax.ShapeDtypeStruct((b, t, 512), BF16),
                   jax.ShapeDtypeStruct((b, t, 512), BF16), jax.ShapeDtypeStruct((b, t, LANES), F32)),
        compiler_params=_cparams(("parallel", "parallel")),
        name="gdn_prep",
    )(cqkv, cqkv, cqkv, gates, conv_w8, gparams)


def _unit_tri_inverses(mats):
    n = mats[0].shape[0]
    r = lax.broadcasted_iota(jnp.int32, (n, n), 0)
    c = lax.broadcasted_iota(jnp.int32, (n, n), 1)
    eye = jnp.where(r == c, 1.0, 0.0)
    ps = [-a for a in mats]
    ts = [eye + p for p in ps]
    for _ in range(5):
        ps = [_mm3(p, p) for p in ps]
        ts = [t + _mm3(t, p) for t, p in zip(ts, ps)]
    return ts


def _gdn_chunk_kernel(q_ref, k_ref, v_ref, bg_ref, l1_ref, l2_ref, u_ref, gl_ref):
    L = GDN_CHUNK
    r = lax.broadcasted_iota(jnp.int32, (L, L), 0)
    c = lax.broadcasted_iota(jnp.int32, (L, L), 1)
    bg = bg_ref[0]
    chains = [(d, h) for d in range(2) for h in range(GDN_HEADS)]
    incl = [(r >= c), (r <= c)]
    strict = [(r > c), (r < c)]
    gcs = [_mm_exact_lhs(jnp.where(m, 1.0, 0.0).astype(BF16), bg) for m in incl]
    gcts = [g.T for g in gcs]
    sl = [slice(h * GDN_DK, (h + 1) * GDN_DK) for h in range(GDN_HEADS)]
    qs = [q_ref[0, :, s] for s in sl]
    ks = [k_ref[0, :, s] for s in sl]
    kfs = [k.astype(F32) for k in ks]
    gcc = [gcs[d][:, 8 + 4 * d + h:9 + 4 * d + h] for d, h in chains]
    gcr = [gcts[d][8 + 4 * d + h:9 + 4 * d + h, :] for d, h in chains]
    dec = [jnp.where(incl[d], jnp.exp(jnp.where(incl[d], gcc[i] - gcr[i], 0.0)), 0.0) for i, (d, h) in enumerate(chains)]
    beta = [bg[:, 4 * d + h:4 * d + h + 1] for d, h in chains]
    kbeta = [kfs[h] * beta[i] for i, (d, h) in enumerate(chains)]
    amat = [jnp.where(strict[d], _mm_nt(kbeta[i], ks[h]) * dec[i], 0.0) for i, (d, h) in enumerate(chains)]
    tinv = _unit_tri_inverses(amat)
    eg = [jnp.exp(g) for g in gcc]
    rhs = [jnp.concatenate([v_ref[0, :, sl[h]].astype(F32) * beta[i], kbeta[i] * eg[i]], axis=1)
           for i, (d, h) in enumerate(chains)]
    sol = [_mm3(t, x) for t, x in zip(tinv, rhs)]
    qk = [_mm_nt(qs[h], ks[h]) for h in range(GDN_HEADS)]
    for i, (d, h) in enumerate(chains):
        last = L - 1 if d == 0 else 0
        glast = gcc[i][last:last + 1, :]
        kdec = kfs[h] * jnp.exp(glast - gcc[i])
        u_ref[0, d, 0, h] = sol[i][:, :GDN_DK]
        l1_ref[0, d, 0, h, 0:L, :] = sol[i][:, GDN_DK:].astype(BF16)
        l1_ref[0, d, 0, h, L:2 * L, :] = (qs[h].astype(F32) * eg[i]).astype(BF16)
        l2_ref[0, d, 0, h, 0:L, :] = (qk[h] * dec[i]).astype(BF16)
        l2_ref[0, d, 0, h, L:, :] = kdec.T.astype(BF16)
        gl_ref[0, d, 0, h:h + 1, :] = jnp.broadcast_to(jnp.exp(glast), (1, LANES))
    for d in range(2):
        gl_ref[0, d, 0, GDN_HEADS:, :] = jnp.zeros((8 - GDN_HEADS, LANES), F32)


def _gdn_chunks(qg, kg, vg, bg):
    b, t, _ = qg.shape
    nc = t // GDN_CHUNK
    L = GDN_CHUNK
    tile = lambda width: pl.BlockSpec((1, L, width), lambda i, j: (i, j, 0))
    return pl.pallas_call(
        _gdn_chunk_kernel,
        grid=(b, nc),
        in_specs=[tile(512), tile(512), tile(512), tile(LANES)],
        out_specs=(pl.BlockSpec((1, 2, 1, 4, 2 * L, GDN_DK), lambda i, j: (i, 0, j, 0, 0, 0)),
                   pl.BlockSpec((1, 2, 1, 4, L + GDN_DK, L), lambda i, j: (i, 0, j, 0, 0, 0)),
                   pl.BlockSpec((1, 2, 1, 4, L, GDN_DK), lambda i, j: (i, 0, j, 0, 0, 0)),
                   pl.BlockSpec((1, 2, 1, 8, LANES), lambda i, j: (i, 0, j, 0, 0))),
        out_shape=(jax.ShapeDtypeStruct((b, 2, nc, 4, 2 * L, GDN_DK), BF16),
                   jax.ShapeDtypeStruct((b, 2, nc, 4, L + GDN_DK, L), BF16),
                   jax.ShapeDtypeStruct((b, 2, nc, 4, L, GDN_DK), F32),
                   jax.ShapeDtypeStruct((b, 2, nc, 8, LANES), F32)),
        compiler_params=_cparams(("parallel", "parallel")),
        name="gdn_chunks",
    )(qg, kg, vg, bg)


def _gdn_scan_kernel(l1f_ref, l2f_ref, uf_ref, glf_ref, l1b_ref, l2b_ref, ub_ref, glb_ref,
                     of_ref, ob_ref, s_ref):
    L = GDN_CHUNK

    @pl.when(pl.program_id(1) == 0)
    def _():
        s_ref[...] = jnp.zeros_like(s_ref)

    dirs = ((l1f_ref, l2f_ref, uf_ref, glf_ref, of_ref), (l1b_ref, l2b_ref, ub_ref, glb_ref, ob_ref))
    chains = [(d, h) for d in range(2) for h in range(GDN_HEADS)]
    ss = [s_ref[i] for i in range(len(chains))]
    r1 = [_mm(dirs[d][0][0, 0, 0, h], ss[i]) for i, (d, h) in enumerate(chains)]
    v_new = [dirs[d][2][0, 0, 0, h] - r1[i][:L] for i, (d, h) in enumerate(chains)]
    r2 = [_mm(dirs[d][1][0, 0, 0, h], v_new[i]) for i, (d, h) in enumerate(chains)]
    for i, (d, h) in enumerate(chains):
        dirs[d][4][0, :, h * GDN_DK:(h + 1) * GDN_DK] = r1[i][L:] + r2[i][:L]
        s_ref[i] = ss[i] * dirs[d][3][0, 0, 0, h:h + 1, :] + r2[i][L:]


def _gdn_scan(l1, l2, u, gl, *, ctx_chunks):
    b, _, nc = l1.shape[:3]
    L = GDN_CHUNK

    def rev(j):
        return jnp.where(j < ctx_chunks, ctx_chunks - 1 - j, nc - 1 - (j - ctx_chunks))

    def spec(a, d):
        blk = (1, 1, 1) + a.shape[3:]
        zeros = (0,) * (a.ndim - 3)
        if d == 0:
            return pl.BlockSpec(blk, lambda i, j: (i, 0, j) + zeros)
        return pl.BlockSpec(blk, lambda i, j: (i, 1, rev(j)) + zeros)

    return pl.pallas_call(
        _gdn_scan_kernel,
        grid=(b, nc),
        in_specs=[spec(l1, 0), spec(l2, 0), spec(u, 0), spec(gl, 0),
                  spec(l1, 1), spec(l2, 1), spec(u, 1), spec(gl, 1)],
        out_specs=(pl.BlockSpec((1, L, 512), lambda i, j: (i, j, 0)),
                   pl.BlockSpec((1, L, 512), lambda i, j: (i, rev(j), 0))),
        out_shape=(jax.ShapeDtypeStruct((b, nc * L, 512), F32), jax.ShapeDtypeStruct((b, nc * L, 512), F32)),
        scratch_shapes=[pltpu.VMEM((2 * GDN_HEADS, GDN_DK, GDN_DK), F32)],
        compiler_params=_cparams(("parallel", "arbitrary")),
        name="gdn_scan",
    )(l1, l2, u, gl, l1, l2, u, gl)


def _merge_kernel(x_ref, h_ref, ya_ref, yb_ref, of_ref, ob_ref, z_ref, mod_ref, gnw_ref, n2w_ref,
                  wg_ref, wb_ref, wo_ref, wr_ref, xo_ref, h2_ref, aff_ref, afft_ref, *, nb, ctx_tiles, d):
    m = _mod_row(mod_ref, nb, ctx_tiles)
    g1, sh2, sc2 = m[:, 2 * d:3 * d], m[:, 3 * d:4 * d], m[:, 4 * d:5 * d]
    o = of_ref[0] + ob_ref[0]
    z = z_ref[0].astype(F32)
    ygs = []
    for h in range(GDN_HEADS):
        sl = slice(h * GDN_DK, (h + 1) * GDN_DK)
        ygs.append(_rms(o[:, sl], gnw_ref[...]) * _silu(z[:, sl]))
    yg = jnp.concatenate(ygs, axis=1).astype(BF16)
    hb = h_ref[0]
    merged = jnp.zeros((TM, d), F32)
    for i, y in enumerate((ya_ref[0], yb_ref[0], yg)):
        gate = _sigmoid(jnp.dot(hb, wg_ref[i], preferred_element_type=F32))
        merged = merged + gate * jnp.dot(y, wb_ref[i], preferred_element_type=F32)
    x = x_ref[0] + g1 * jnp.dot(merged.astype(BF16), wo_ref[...], preferred_element_type=F32)
    xo_ref[0] = x
    h2 = _rms(x, n2w_ref[...]) * (1.0 + sc2) + sh2
    h2b = h2.astype(BF16)
    h2_ref[0] = h2b
    logits = _mm3(h2, wr_ref[...])
    lane = lax.broadcasted_iota(jnp.int32, logits.shape, 1)
    logits = jnp.where(lane < N_EXPERTS, logits, -jnp.inf)
    e = jnp.exp(logits - jnp.max(logits, axis=-1, keepdims=True))
    aff = e / jnp.sum(e, axis=-1, keepdims=True)
    aff_ref[0] = aff
    afft_ref[0] = aff.T[:N_EXPERTS]


def _merge(xs, h, ya, yb, o_f, o_b, z, mod, gnw, n2w, wg, wb, wo, wr, *, nb, ctx_tiles):
    b, t, d = xs.shape
    nt = t // TM
    tile = lambda width: pl.BlockSpec((1, TM, width), lambda i, j: (i, j, 0))
    whole = lambda a: pl.BlockSpec(a.shape, lambda i, j: (0,) * a.ndim)
    return pl.pallas_call(
        functools.partial(_merge_kernel, nb=nb, ctx_tiles=ctx_tiles, d=d),
        grid=(b, nt),
        in_specs=[tile(d), tile(d), tile(512), tile(512), tile(512), tile(512), tile(512),
                  whole(mod), whole(gnw), whole(n2w), whole(wg), whole(wb), whole(wo), whole(wr)],
        out_specs=(tile(d), tile(d), tile(LANES), pl.BlockSpec((1, N_EXPERTS, TM), lambda i, j: (i, 0, j))),
        out_shape=(jax.ShapeDtypeStruct((b, t, d), F32), jax.ShapeDtypeStruct((b, t, d), BF16),
                   jax.ShapeDtypeStruct((b, t, LANES), F32), jax.ShapeDtypeStruct((b, N_EXPERTS, t), F32)),
        compiler_params=_cparams(("parallel", "parallel")),
        name="merge",
    )(xs, h, ya, yb, o_f, o_b, z, mod, gnw, n2w, wg, wb, wo, wr)


ROW_ALIGN = 16
SLOT_GROUP = 64
FFN_CHUNKS = 3


def _moe_rows(caps, nt):
    chunk = -(-(sum(caps) + (ROW_ALIGN - 1) * nt) // (ROW_ALIGN * FFN_CHUNKS)) * ROW_ALIGN
    return chunk, FFN_CHUNKS * chunk + TM


def _route_kernel(afft_ref, post_ref, posc_ref, starts_ref, *, ctx_tiles, nt, cap_ctx, cap_lat):
    a = afft_ref[0]
    bits = pltpu.bitcast(a, jnp.int32)
    lane = lax.broadcasted_iota(jnp.int32, a.shape, 1)
    isctx = lane < ctx_tiles * TM

    def bisect(i, carry):
        tc, tl = carry
        bit = jnp.left_shift(jnp.int32(1), 30 - i)
        cc, cl = tc | bit, tl | bit
        ge = bits >= jnp.where(isctx, cc, cl)
        n_c = jnp.sum(jnp.where(jnp.logical_and(ge, isctx), 1.0, 0.0), axis=-1, keepdims=True)
        n_l = jnp.sum(jnp.where(jnp.logical_and(ge, jnp.logical_not(isctx)), 1.0, 0.0), axis=-1, keepdims=True)
        return jnp.where(n_c >= cap_ctx, cc, tc), jnp.where(n_l >= cap_lat, cl, tl)

    zero = jnp.zeros((N_EXPERTS, 1), jnp.int32)
    tc, tl = lax.fori_loop(0, 31, bisect, (zero, zero))
    thr = jnp.where(isctx, tc, tl)
    gt = bits > thr
    eq = bits == thr
    n_gt_c = jnp.sum(jnp.where(jnp.logical_and(gt, isctx), 1.0, 0.0), axis=-1, keepdims=True)
    n_gt_l = jnp.sum(jnp.where(jnp.logical_and(gt, jnp.logical_not(isctx)), 1.0, 0.0), axis=-1, keepdims=True)
    r = lax.broadcasted_iota(jnp.int32, (TM, TM), 0)
    c = lax.broadcasted_iota(jnp.int32, (TM, TM), 1)
    upper = jnp.where(r < c, 1.0, 0.0).astype(BF16)
    lane_t = lax.broadcasted_iota(jnp.int32, (N_EXPERTS, LANES), 1)
    starts = jnp.zeros((N_EXPERTS, LANES), F32)
    start = jnp.zeros((N_EXPERTS, 1), F32)
    eq_seen = jnp.zeros((N_EXPERTS, 1), F32)
    for j in range(nt):
        sl = slice(j * TM, (j + 1) * TM)
        if j == ctx_tiles:
            eq_seen = jnp.zeros((N_EXPERTS, 1), F32)
        need = (cap_ctx - n_gt_c) if j < ctx_tiles else (cap_lat - n_gt_l)
        eq_j = jnp.where(eq[:, sl], 1.0, 0.0)
        rank = eq_seen + _mm(eq_j, upper)
        eq_seen = eq_seen + jnp.sum(eq_j, axis=-1, keepdims=True)
        sel = jnp.logical_or(gt[:, sl], jnp.logical_and(eq[:, sl], rank < need))
        sel_f = jnp.where(sel, 1.0, 0.0)
        pos = jnp.where(sel, start + _mm(sel_f, upper), -1.0)
        post_ref[0, :, sl] = pos
        posc_ref[0, sl, :] = jnp.concatenate([pos, jnp.full((LANES - N_EXPERTS, TM), -1.0, F32)], axis=0).T
        starts = jnp.where(lane_t == j, start, starts)
        cnt = jnp.sum(sel_f, axis=-1, keepdims=True)
        start = start + jnp.floor((cnt + (ROW_ALIGN - 1)) * (1.0 / ROW_ALIGN)) * ROW_ALIGN
    starts_ref[0] = jnp.where(lane_t == nt, start, starts).astype(jnp.int32)


def _route(afft, *, ctx_tiles, cap_ctx, cap_lat):
    b, e, t = afft.shape
    nt = t // TM
    return pl.pallas_call(
        functools.partial(_route_kernel, ctx_tiles=ctx_tiles, nt=nt, cap_ctx=cap_ctx, cap_lat=cap_lat),
        grid=(b,),
        in_specs=[pl.BlockSpec((1, e, t), lambda i: (i, 0, 0))],
        out_specs=(pl.BlockSpec((1, e, t), lambda i: (i, 0, 0)),
                   pl.BlockSpec((1, t, LANES), lambda i: (i, 0, 0)),
                   pl.BlockSpec((1, e, LANES), lambda i: (i, 0, 0))),
        out_shape=(jax.ShapeDtypeStruct((b, e, t), F32), jax.ShapeDtypeStruct((b, t, LANES), F32),
                   jax.ShapeDtypeStruct((b, e, LANES), jnp.int32)),
        compiler_params=_cparams(("parallel",)),
        name="moe_route",
    )(afft)


def _moe_ffn_kernel(starts_ref, h2_ref, post_ref, wg_ref, wu_ref, wd_ref, y_ref, xin_ref, *, nt, unroll, chunk):
    b, e = pl.program_id(0), pl.program_id(1)
    xin_ref[...] = jnp.zeros_like(xin_ref)
    slot = lax.broadcasted_iota(jnp.int32, (SLOT_GROUP, TM), 0).astype(F32)

    def picks(j, g):
        base = (b * N_EXPERTS + e) * LANES + j
        off = starts_ref[base]
        tok = pl.multiple_of(j * TM, TM)
        pos = post_ref[0, pl.ds(e, 1), pl.ds(tok, TM)] - (off + g * SLOT_GROUP).astype(F32)
        onehot = jnp.where(pos == slot, 1.0, 0.0).astype(BF16)
        blk = jnp.dot(onehot, h2_ref[0, pl.ds(tok, TM), :], preferred_element_type=F32)
        return pl.multiple_of(off + g * SLOT_GROUP, ROW_ALIGN), starts_ref[base + 1] - off - g * SLOT_GROUP, blk

    def first_groups(i, carry):
        blocks = [picks(i * unroll + k, 0) for k in range(unroll)]
        for row0, _, blk in blocks:
            xin_ref[pl.ds(row0, SLOT_GROUP), :] = blk
        return carry

    lax.fori_loop(0, nt // unroll, first_groups, 0)

    def more_groups(j, carry):
        def group(g, carry):
            row0, n_real, blk = picks(j, g)
            keep = lax.broadcasted_iota(jnp.int32, blk.shape, 0) < n_real
            xin_ref[pl.ds(row0, SLOT_GROUP), :] = jnp.where(keep, blk, xin_ref[pl.ds(row0, SLOT_GROUP), :])
            return carry

        base = (b * N_EXPERTS + e) * LANES + j
        n_groups = (starts_ref[base + 1] - starts_ref[base] + (SLOT_GROUP - 1)) // SLOT_GROUP
        return lax.fori_loop(1, n_groups, group, carry)

    lax.fori_loop(0, nt, more_groups, 0)

    def ffn(c, carry):
        rows = pl.ds(pl.multiple_of(c * chunk, ROW_ALIGN), chunk)
        x = xin_ref[rows, :].astype(BF16)
        hid = _silu(jnp.dot(x, wg_ref[0], preferred_element_type=F32)) * jnp.dot(x, wu_ref[0], preferred_element_type=F32)
        y_ref[0, 0, rows, :] = jnp.dot(hid.astype(BF16), wd_ref[0], preferred_element_type=F32).astype(BF16)
        return carry

    lax.fori_loop(0, FFN_CHUNKS, ffn, 0)
    y_ref[0, 0, FFN_CHUNKS * chunk:, :] = jnp.zeros((TM, y_ref.shape[3]), BF16)


def _moe_ffn(starts, h2, post, wg, wu, wd, *, chunk, rows):
    b, t, d = h2.shape
    nt = t // TM
    e, _, ff = wg.shape
    grid_spec = pltpu.PrefetchScalarGridSpec(
        num_scalar_prefetch=1,
        grid=(b, e),
        in_specs=[pl.BlockSpec((1, t, d), lambda i, k, s: (i, 0, 0), pipeline_mode=pl.Buffered(1)),
                  pl.BlockSpec((1, e, t), lambda i, k, s: (i, 0, 0)),
                  pl.BlockSpec((1, d, ff), lambda i, k, s: (k, 0, 0)),
                  pl.BlockSpec((1, d, ff), lambda i, k, s: (k, 0, 0)),
                  pl.BlockSpec((1, ff, d), lambda i, k, s: (k, 0, 0))],
        out_specs=pl.BlockSpec((1, 1, rows, d), lambda i, k, s: (i, k, 0, 0)),
        scratch_shapes=[pltpu.VMEM((rows, d), F32)],
    )
    return pl.pallas_call(
        functools.partial(_moe_ffn_kernel, nt=nt, unroll=max(u for u in range(1, 12) if nt % u == 0), chunk=chunk),
        grid_spec=grid_spec,
        out_shape=jax.ShapeDtypeStruct((b, e, rows, d), BF16),
        compiler_params=pltpu.CompilerParams(dimension_semantics=("parallel", "arbitrary"),
                                             vmem_limit_bytes=MOE_FFN_VMEM_LIMIT),
        name="moe_ffn",
    )(starts, h2, post, wg, wu, wd)


def _moe_combine_kernel(starts_ref, y_ref, posc_ref, aff_ref, x_ref, g2_ref, o_ref, *, nb, ctx_tiles):
    b, j = pl.program_id(0), pl.program_id(2)
    slot = lax.broadcasted_iota(jnp.int32, (TM, TM), 1).astype(F32)
    posc, aff = posc_ref[0], aff_ref[0]
    acc = jnp.zeros(o_ref.shape[1:], F32)
    for e in range(N_EXPERTS):
        off = pl.multiple_of(starts_ref[(b * N_EXPERTS + e) * LANES + j], ROW_ALIGN)
        pos = posc[:, e:e + 1]
        onehot = jnp.where(pos - off.astype(F32) == slot, 1.0, 0.0).astype(BF16)
        w = jnp.where(pos >= 0.0, aff[:, e:e + 1], 0.0)
        acc = acc + w * jnp.dot(onehot, y_ref[0, e, pl.ds(off, TM), :], preferred_element_type=F32)
    row = jnp.where(j < ctx_tiles, nb, b)
    o_ref[0] = x_ref[0] + g2_ref[pl.ds(row, 1), :] * acc


def _moe_combine(starts, y, posc, aff, xs, g2, *, nb, ctx_tiles, dsplit=2):
    b, t, d = xs.shape
    e, rows = y.shape[1:3]
    dq = d // dsplit
    grid_spec = pltpu.PrefetchScalarGridSpec(
        num_scalar_prefetch=1,
        grid=(b, dsplit, t // TM),
        in_specs=[pl.BlockSpec((1, e, rows, dq), lambda i, q, j, s: (i, 0, 0, q), pipeline_mode=pl.Buffered(1)),
                  pl.BlockSpec((1, TM, LANES), lambda i, q, j, s: (i, j, 0)),
                  pl.BlockSpec((1, TM, LANES), lambda i, q, j, s: (i, j, 0)),
                  pl.BlockSpec((1, TM, dq), lambda i, q, j, s: (i, j, q)),
                  pl.BlockSpec((MOD_ROWS, dq), lambda i, q, j, s: (0, q))],
        out_specs=pl.BlockSpec((1, TM, dq), lambda i, q, j, s: (i, j, q)),
    )
    return pl.pallas_call(
        functools.partial(_moe_combine_kernel, nb=nb, ctx_tiles=ctx_tiles),
        grid_spec=grid_spec,
        out_shape=jax.ShapeDtypeStruct((b, t, d), F32),
        compiler_params=_cparams(("parallel", "parallel", "parallel")),
        name="moe_combine",
    )(starts, y, posc, aff, xs, g2)


def _lambda_init(layer_idx):
    return 0.8 - 0.6 * math.exp(-0.3 * layer_idx)


def _rope_tables(n_lat, n_ctx):
    rows = n_lat // GRID_W
    row = jnp.repeat(jnp.arange(rows, dtype=F32), GRID_W)
    col = jnp.tile(jnp.arange(GRID_W, dtype=F32), rows)
    axis_dim = HEAD_DIM // 2
    inv_freq = ROPE_THETA ** (-jnp.arange(0, axis_dim, 2, dtype=F32) / axis_dim)
    ang_r, ang_c = row[:, None] * inv_freq, col[:, None] * inv_freq
    cos = jnp.concatenate([jnp.cos(ang_r)] * 2 + [jnp.cos(ang_c)] * 2, axis=-1)
    sin = jnp.concatenate([-jnp.sin(ang_r), jnp.sin(ang_r), -jnp.sin(ang_c), jnp.sin(ang_c)], axis=-1)
    cos = jnp.concatenate([jnp.ones((n_ctx, HEAD_DIM), F32), cos], axis=0)
    sin = jnp.concatenate([jnp.zeros((n_ctx, HEAD_DIM), F32), sin], axis=0)
    return jnp.tile(cos, (1, 2)), jnp.tile(sin, (1, 2))


def kernel(x, c, ctx, c_ctx, w_mod, b_mod, norm1_w, norm2_w, w_in, gqa_q_norm, gqa_k_norm, diff_q_norm,
           diff_k_norm, diff_lambda_q1, diff_lambda_k1, diff_lambda_q2, diff_lambda_k2, diff_subln,
           gdn_conv_w, gdn_a_log, gdn_dt_bias, gdn_norm_w, w_merge_gate, w_branch, w_out, w_router,
           w_exp_gate, w_exp_up, w_exp_down):
    nb, n_lat, d = x.shape
    n_ctx = ctx.shape[1]
    depth = w_mod.shape[0]
    assert n_ctx % TM == 0 and n_lat % TM == 0 and nb < MOD_ROWS and d == 1024
    ctx_tiles = n_ctx // TM
    ctx_chunks = n_ctx // GDN_CHUNK

    cos_t, sin_t = _rope_tables(n_lat, n_ctx)
    cc = jnp.zeros((MOD_ROWS, d), F32).at[:nb].set(c).at[nb].set(c_ctx)
    mods = _modulation(cc, w_mod, b_mod)
    half = lax.broadcasted_iota(jnp.int32, (LANES, LANES), 0) // HEAD_DIM
    bd = (half == half.T).astype(BF16)

    xs = jnp.concatenate([ctx, x], axis=1)
    scale = HEAD_DIM ** -0.5
    for li in range(depth):
        lam_init = _lambda_init(li)
        w_in_p = jnp.pad(w_in[li], ((0, 0), (0, 4480 - w_in.shape[-1]))).astype(BF16)
        qkn = jnp.stack([jnp.tile(gqa_q_norm[li] * scale, 2), jnp.tile(gqa_k_norm[li], 2),
                         jnp.tile(diff_q_norm[li] * scale, 2), jnp.tile(diff_k_norm[li], 2)])
        (h, qa, ka, va, qb, kb, vb, cqkv, z, gates) = _inproj(
            xs, mods[li], norm1_w[li][None, :], w_in_p, cos_t, sin_t, qkn, bd, nb=nb, ctx_tiles=ctx_tiles)

        ya = _gqa_attention(qa, ka, va, n_ctx=n_ctx)
        lam_rows = jnp.stack([jnp.concatenate([diff_lambda_q1[li], diff_lambda_q2[li]]),
                              jnp.concatenate([diff_lambda_k1[li], diff_lambda_k2[li]])])
        yb = _diff_attention(qb, kb, vb, lam_rows, diff_subln[li][None, :], n_ctx=n_ctx, lam_init=lam_init)

        conv_w8 = jnp.pad(gdn_conv_w[li], ((0, 5), (0, 0)))
        gparams = jnp.zeros((2, LANES), F32)
        gparams = gparams.at[0, 8:16].set(gdn_a_log[li].reshape(-1)).at[1, 8:16].set(gdn_dt_bias[li].reshape(-1))
        qg, kg, vg, bg = _gdn_prep(cqkv, gates, conv_w8, gparams, ctx_tiles=ctx_tiles)
        l1, l2, u, gl = _gdn_chunks(qg, kg, vg, bg)
        o_f, o_b = _gdn_scan(l1, l2, u, gl, ctx_chunks=ctx_chunks)

        wr = jnp.pad(w_router[li], ((0, 0), (0, LANES - N_EXPERTS)))
        xs, h2, aff, afft = _merge(xs, h, ya, yb, o_f, o_b, z, mods[li], gdn_norm_w[li][None, :],
                                   norm2_w[li][None, :], w_merge_gate[li].astype(BF16), w_branch[li].astype(BF16),
                                   w_out[li].astype(BF16), wr, nb=nb, ctx_tiles=ctx_tiles)

        caps = (CAPACITY_FACTOR * n_ctx // N_EXPERTS, CAPACITY_FACTOR * n_lat // N_EXPERTS)
        chunk, rows = _moe_rows(caps, (n_ctx + n_lat) // TM)
        post, posc, starts = _route(afft, ctx_tiles=ctx_tiles, cap_ctx=caps[0], cap_lat=caps[1])
        starts = starts.reshape(-1)
        y = _moe_ffn(starts, h2, post, w_exp_gate[li].astype(BF16), w_exp_up[li].astype(BF16),
                     w_exp_down[li].astype(BF16), chunk=chunk, rows=rows)
        xs = _moe_combine(starts, y, posc, aff, xs, mods[li, :, 5 * d:], nb=nb, ctx_tiles=ctx_tiles)
    return xs[:, n_ctx:]
```

```python
import functools
import math

import jax
import jax.numpy as jnp
from jax import lax
from jax.experimental import pallas as pl
from jax.experimental.pallas import tpu as pltpu

F32 = jnp.float32
BF16 = jnp.bfloat16

LANES = 128
TM = 256
GRID_W = 64
HEAD_DIM = 64
GQA_HEADS = 8
GQA_KV_HEADS = 2
DIFF_HEADS = 4
GDN_HEADS = 4
GDN_DK = 128
GDN_CHUNK = 64
N_EXPERTS = 16
CAPACITY_FACTOR = 2
ROPE_THETA = 10000.0
EPS = 1e-6
ONES_ROWS = 16
MOD_ROWS = 16
VMEM_LIMIT = 56 * 1024 * 1024
MOE_FFN_VMEM_LIMIT = 60 * 1024 * 1024


def _cparams(sem):
    return pltpu.CompilerParams(dimension_semantics=sem, vmem_limit_bytes=VMEM_LIMIT)


def _mm(a, b):
    return jnp.dot(a.astype(BF16), b.astype(BF16), preferred_element_type=F32)


def _mm_nt(a, b):
    return lax.dot_general(a.astype(BF16), b.astype(BF16), (((1,), (1,)), ((), ())),
                           preferred_element_type=F32)


def _split2(a):
    hi = a.astype(BF16)
    lo = (a - hi.astype(F32)).astype(BF16)
    return hi, lo


def _mm3(a, b):
    ah, al = _split2(a)
    bh, bl = _split2(b)
    return _mm(ah, bh) + _mm(ah, bl) + _mm(al, bh)


def _mm_exact_lhs(a_bf16, b):
    b1 = b.astype(BF16)
    r1 = b - b1.astype(F32)
    b2 = r1.astype(BF16)
    b3 = (r1 - b2.astype(F32)).astype(BF16)
    return _mm(a_bf16, b1) + _mm(a_bf16, b2) + _mm(a_bf16, b3)


def _sigmoid(x):
    return 1.0 / (1.0 + jnp.exp(-x))


def _silu(x):
    return x * _sigmoid(x)


def _mod_kernel(a_ref, w_ref, b_ref, o_ref):
    a = _silu(a_ref[...])
    o_ref[0] = _mm3(a, w_ref[0]) + b_ref[0]


def _modulation(cc, w_mod, b_mod):
    depth, d, d6 = w_mod.shape
    nj = d6 // d
    return pl.pallas_call(
        _mod_kernel,
        grid=(depth, nj),
        in_specs=[pl.BlockSpec((MOD_ROWS, d), lambda l, j: (0, 0)),
                  pl.BlockSpec((1, d, d), lambda l, j: (l, 0, j)),
                  pl.BlockSpec((1, 1, d), lambda l, j: (l, 0, j))],
        out_specs=pl.BlockSpec((1, MOD_ROWS, d), lambda l, j: (l, 0, j)),
        out_shape=jax.ShapeDtypeStruct((depth, MOD_ROWS, d6), F32),
        compiler_params=_cparams(("parallel", "parallel")),
        name="modulation",
    )(cc, w_mod, b_mod.reshape(depth, 1, d6))


def _mod_row(mod_ref, nb, ctx_tiles):
    b = pl.program_id(0)
    t = pl.program_id(1)
    row = jnp.where(t < ctx_tiles, nb, b)
    return mod_ref[pl.ds(row, 1), :]


def _rms(x, w):
    return x * lax.rsqrt(jnp.mean(x * x, axis=-1, keepdims=True) + EPS) * w


def _half_norm_rope(t, w, bd, cos, sin):
    ssq = _mm(t * t, bd)
    tn = t * lax.rsqrt(ssq * (1.0 / HEAD_DIM) + EPS) * w
    lane = lax.broadcasted_iota(jnp.int32, tn.shape, 1)
    partner = jnp.where((lane % 32) < 16, pltpu.roll(tn, LANES - 16, axis=1), pltpu.roll(tn, 16, axis=1))
    return tn * cos + partner * sin


def _inproj_kernel(x_ref, mod_ref, nw_ref, w_ref, cos_ref, sin_ref, qkn_ref, bd_ref,
                   h_ref, qa_ref, ka_ref, va_ref, qb_ref, kb_ref, vb_ref, c_ref, z_ref, g_ref,
                   *, nb, ctx_tiles, d):
    m = _mod_row(mod_ref, nb, ctx_tiles)
    sh, sc = m[:, 0:d], m[:, d:2 * d]
    h = _rms(x_ref[0], nw_ref[...]) * (1.0 + sc) + sh
    hb = h.astype(BF16)
    h_ref[0] = hb
    cos, sin, bd = cos_ref[...], sin_ref[...], bd_ref[...]

    def proj(lo, width):
        return jnp.dot(hb, w_ref[:, lo:lo + width], preferred_element_type=F32)

    ones_rows = jnp.ones((ONES_ROWS, TM), BF16)
    for g in range(4):
        t = _half_norm_rope(proj(g * LANES, LANES), qkn_ref[0:1, :], bd, cos, sin)
        qa_ref[0, g * LANES:(g + 1) * LANES, :] = t.T.astype(BF16)
    ka_ref[0] = _half_norm_rope(proj(512, LANES), qkn_ref[1:2, :], bd, cos, sin).astype(BF16)
    vt = proj(640, LANES).T.astype(BF16)
    for j in range(GQA_KV_HEADS):
        va_ref[0, j, 0:HEAD_DIM, :] = vt[j * HEAD_DIM:(j + 1) * HEAD_DIM]
        va_ref[0, j, HEAD_DIM:HEAD_DIM + ONES_ROWS, :] = ones_rows
    for g in range(4):
        t = _half_norm_rope(proj(768 + g * LANES, LANES), qkn_ref[2:3, :], bd, cos, sin)
        qb_ref[0, g * LANES:(g + 1) * LANES, :] = t.T.astype(BF16)
        t = proj(1280 + g * LANES, LANES)
        kb_ref[0, :, g * LANES:(g + 1) * LANES] = _half_norm_rope(t, qkn_ref[3:4, :], bd, cos, sin).astype(BF16)
        vb_ref[0, g, 0:LANES, :] = proj(1792 + g * LANES, LANES).T.astype(BF16)
        vb_ref[0, g, LANES:LANES + ONES_ROWS, :] = ones_rows
    for g in range(3):
        c_ref[0, :, g * 512:(g + 1) * 512] = proj(2304 + g * 512, 512)
    z_ref[0] = proj(3840, 512).astype(BF16)
    g_ref[0] = proj(4352, LANES)


def _inproj(xs, mod, nw, w_in_p, cos_t, sin_t, qkn, bd, *, nb, ctx_tiles):
    b, t, d = xs.shape
    nt = t // TM
    tile = lambda width: pl.BlockSpec((1, TM, width), lambda i, j: (i, j, 0))
    whole = lambda a: pl.BlockSpec(a.shape, lambda i, j: (0,) * a.ndim)
    out_shape = (
        jax.ShapeDtypeStruct((b, t, d), BF16),
        jax.ShapeDtypeStruct((b, 512, t), BF16),
        jax.ShapeDtypeStruct((b, t, LANES), BF16),
        jax.ShapeDtypeStruct((b, 2, HEAD_DIM + ONES_ROWS, t), BF16),
        jax.ShapeDtypeStruct((b, 512, t), BF16),
        jax.ShapeDtypeStruct((b, t, 512), BF16),
        jax.ShapeDtypeStruct((b, 4, LANES + ONES_ROWS, t), BF16),
        jax.ShapeDtypeStruct((b, t, 1536), F32),
        jax.ShapeDtypeStruct((b, t, 512), BF16),
        jax.ShapeDtypeStruct((b, t, LANES), F32),
    )
    out_specs = (
        tile(d),
        pl.BlockSpec((1, 512, TM), lambda i, j: (i, 0, j)),
        tile(LANES),
        pl.BlockSpec((1, 2, HEAD_DIM + ONES_ROWS, TM), lambda i, j: (i, 0, 0, j)),
        pl.BlockSpec((1, 512, TM), lambda i, j: (i, 0, j)),
        tile(512),
        pl.BlockSpec((1, 4, LANES + ONES_ROWS, TM), lambda i, j: (i, 0, 0, j)),
        tile(1536), tile(512), tile(LANES),
    )
    return pl.pallas_call(
        functools.partial(_inproj_kernel, nb=nb, ctx_tiles=ctx_tiles, d=d),
        grid=(b, nt),
        in_specs=[tile(d), whole(mod), whole(nw), whole(w_in_p),
                  pl.BlockSpec((TM, LANES), lambda i, j: (j, 0)),
                  pl.BlockSpec((TM, LANES), lambda i, j: (j, 0)),
                  whole(qkn), whole(bd)],
        out_specs=out_specs,
        out_shape=out_shape,
        compiler_params=_cparams(("parallel", "parallel")),
        name="inproj",
    )(xs, mod, nw, w_in_p, cos_t, sin_t, qkn, bd)


def _softmax_step(st, vt, mx, acc):
    mnew = jnp.maximum(mx, jnp.max(st, axis=0, keepdims=True))
    p = jnp.exp(st - mnew).astype(BF16)
    acc = acc * jnp.exp(mx - mnew) + jnp.dot(vt, p, preferred_element_type=F32)
    return mnew, acc


def _attend(w, k_at, vt_at, n_ctx, n_groups, tk, s_refs, vrows):
    unroll = len(s_refs)

    def scores(c):
        return jnp.dot(k_at(lat(c), tk), w, preferred_element_type=F32)

    def lat(c):
        return pl.multiple_of(n_ctx + c * tk, TM)

    mx = jnp.full((1, w.shape[1]), -jnp.inf, F32)
    acc = jnp.zeros((vrows, w.shape[1]), F32)
    mx, acc = _softmax_step(jnp.dot(k_at(0, n_ctx), w, preferred_element_type=F32), vt_at(0, n_ctx), mx, acc)
    if n_groups == 0:
        return acc

    def group(i, carry, prefetch):
        mx, acc = carry
        for u in range(unroll):
            c = i * unroll + u
            if u + 1 < unroll or prefetch:
                s_refs[(u + 1) % unroll][...] = scores(c + 1)
            mx, acc = _softmax_step(s_refs[u][...], vt_at(lat(c), tk), mx, acc)
        return mx, acc

    s_refs[0][...] = scores(0)
    carry = lax.fori_loop(0, n_groups - 1, functools.partial(group, prefetch=True), (mx, acc))
    return group(n_groups - 1, carry, False)[1]


ATTN_Q_BLOCKS = 2
ATTN_UNROLL = 4
ATTN_TK = 512


def _attn_chunk(n_lat):
    tk = min(ATTN_TK, max(TM, n_lat // ATTN_UNROLL))
    unroll = min(ATTN_UNROLL, n_lat // tk)
    assert n_lat % (unroll * tk) == 0 and tk % TM == 0 and n_lat % (ATTN_Q_BLOCKS * TM) == 0
    return tk, unroll, n_lat // (unroll * tk)


def _attention_call(kern, name, q_t, others, other_specs, *, n_ctx, heads):
    b, _, t = q_t.shape
    n_lat = t - n_ctx
    tk, unroll, n_groups = _attn_chunk(n_lat)
    nq, ctx_tiles = ATTN_Q_BLOCKS, n_ctx // TM
    tq = nq * TM
    q_specs = [pl.BlockSpec((1, LANES, TM), functools.partial(lambda i, g, j, u: (i, g, ctx_tiles + nq * j + u), u=u))
               for u in range(nq)]
    lat = pl.pallas_call(
        functools.partial(kern, nq=nq, n_ctx=n_ctx, n_groups=n_groups, tk=tk),
        grid=(b, heads, n_lat // tq),
        in_specs=q_specs + other_specs(t),
        out_specs=pl.BlockSpec((1, tq, LANES), lambda i, g, j: (i, j, g)),
        out_shape=jax.ShapeDtypeStruct((b, n_lat, 512), BF16),
        scratch_shapes=[pltpu.VMEM((tk, 2 * tq), F32)] * unroll,
        compiler_params=_cparams(("parallel", "parallel", "parallel")),
        name=name,
    )(*([q_t] * nq), *others)
    ctx = pl.pallas_call(
        functools.partial(kern, nq=1, n_ctx=n_ctx, n_groups=0, tk=tk),
        grid=(b, heads, ctx_tiles),
        in_specs=[pl.BlockSpec((1, LANES, TM), lambda i, g, j: (i, g, j))] + other_specs(n_ctx),
        out_specs=pl.BlockSpec((1, TM, LANES), lambda i, g, j: (i, j, g)),
        out_shape=jax.ShapeDtypeStruct((b, n_ctx, 512), BF16),
        compiler_params=_cparams(("parallel", "parallel", "parallel")),
        name=name + "_ctx",
    )(q_t, *others)
    return ctx, lat


def _gqa_kernel(*refs, nq, n_ctx, n_groups, tk):
    qt_refs, (k_ref, vt_ref, o_ref), s_refs = refs[:nq], refs[nq:nq + 3], refs[nq + 3:]
    tq = nq * TM
    top = jnp.concatenate([r[0, :HEAD_DIM] for r in qt_refs] + [r[0, HEAD_DIM:] for r in qt_refs], axis=1)
    zero = jnp.zeros_like(top)
    w = jnp.where(pl.program_id(1) < 2, jnp.concatenate([top, zero], axis=0), jnp.concatenate([zero, top], axis=0))
    vrows = HEAD_DIM + ONES_ROWS
    acc = _attend(w, lambda off, n: k_ref[0, pl.ds(off, n), :], lambda off, n: vt_ref[0, 0, :, pl.ds(off, n)],
                  n_ctx, n_groups, tk, s_refs, vrows)
    o = acc[:HEAD_DIM] / acc[HEAD_DIM:HEAD_DIM + 1]
    o_ref[0] = jnp.concatenate([o[:, :tq], o[:, tq:]], axis=0).T.astype(BF16)


def _gqa_attention(qat, ka, vat, *, n_ctx):
    vrows = vat.shape[2]
    specs = lambda keys: [pl.BlockSpec((1, keys, LANES), lambda i, g, j: (i, 0, 0)),
                          pl.BlockSpec((1, 1, vrows, keys), lambda i, g, j: (i, g // 2, 0, 0))]
    return _attention_call(_gqa_kernel, "gqa_attention", qat, (ka, vat), specs, n_ctx=n_ctx, heads=4)


def _diff_kernel(*refs, nq, n_ctx, n_groups, tk, lam_init):
    qt_refs, (k_ref, vt_ref, lam_ref, sw_ref, o_ref), s_refs = refs[:nq], refs[nq:nq + 5], refs[nq + 5:]
    tq = nq * TM
    qt = jnp.concatenate([r[0] for r in qt_refs], axis=1)
    zero = jnp.zeros((HEAD_DIM, tq), BF16)
    w = jnp.concatenate([jnp.concatenate([qt[:HEAD_DIM], zero], axis=0),
                         jnp.concatenate([zero, qt[HEAD_DIM:]], axis=0)], axis=1)
    vrows = LANES + ONES_ROWS
    acc = _attend(w, lambda off, n: k_ref[0, pl.ds(off, n), :], lambda off, n: vt_ref[0, 0, :, pl.ds(off, n)],
                  n_ctx, n_groups, tk, s_refs, vrows)
    prod = lam_ref[0:1, :] * lam_ref[1:2, :]
    low = lax.broadcasted_iota(jnp.int32, prod.shape, 1) < HEAD_DIM
    l1 = jnp.sum(jnp.where(low, prod, 0.0), axis=-1, keepdims=True)
    l2 = jnp.sum(jnp.where(low, 0.0, prod), axis=-1, keepdims=True)
    lam = jnp.exp(l1) - jnp.exp(l2) + lam_init
    o1 = acc[:LANES, :tq] / acc[LANES:LANES + 1, :tq]
    o2 = acc[:LANES, tq:] / acc[LANES:LANES + 1, tq:]
    o = (o1 - lam * o2).T
    o_ref[0] = (_rms(o, sw_ref[...]) * (1.0 - lam_init)).astype(BF16)


def _diff_attention(qbt, kb, vbt, lam_rows, subln, *, n_ctx, lam_init):
    vrows = vbt.shape[2]
    specs = lambda keys: [pl.BlockSpec((1, keys, LANES), lambda i, g, j: (i, 0, g)),
                          pl.BlockSpec((1, 1, vrows, keys), lambda i, g, j: (i, g, 0, 0)),
                          pl.BlockSpec(lam_rows.shape, lambda i, g, j: (0, 0)),
                          pl.BlockSpec(subln.shape, lambda i, g, j: (0, 0))]
    return _attention_call(functools.partial(_diff_kernel, lam_init=lam_init), "diff_attention", qbt,
                           (kb, vbt, lam_rows, subln), specs, n_ctx=n_ctx, heads=DIFF_HEADS)


def _gdn_prep_kernel(c_ref, cp_ref, cn_ref, g_ref, cw_ref, gp_ref, q_ref, k_ref, v_ref, bg_ref, *, ctx_tiles, nt):
    t = pl.program_id(1)
    x = c_ref[0]
    first = jnp.logical_or(t == 0, t == ctx_tiles)
    last = jnp.logical_or(t == ctx_tiles - 1, t == nt - 1)
    prev_row = jnp.where(first, 0.0, cp_ref[0, 7:8, :])
    next_row = jnp.where(last, 0.0, cn_ref[0, 0:1, :])
    ridx = lax.broadcasted_iota(jnp.int32, x.shape, 0)
    xm = jnp.where(ridx == 0, prev_row, pltpu.roll(x, 1, axis=0))
    xp = jnp.where(ridx == TM - 1, next_row, pltpu.roll(x, TM - 1, axis=0))
    y = _silu(xm * cw_ref[0:1, :] + x * cw_ref[1:2, :] + xp * cw_ref[2:3, :])
    for h in range(GDN_HEADS):
        sl = slice(h * GDN_DK, (h + 1) * GDN_DK)
        qh = y[:, sl]
        q_ref[0, :, sl] = (qh * lax.rsqrt(jnp.sum(qh * qh, axis=-1, keepdims=True) + EPS)
                           * (GDN_DK ** -0.5)).astype(BF16)
        kh = y[:, 512 + h * GDN_DK:512 + (h + 1) * GDN_DK]
        k_ref[0, :, sl] = (kh * lax.rsqrt(jnp.sum(kh * kh, axis=-1, keepdims=True) + EPS)).astype(BF16)
    v_ref[0] = y[:, 1024:1536].astype(BF16)
    gx = g_ref[0]
    lane = lax.broadcasted_iota(jnp.int32, gx.shape, 1)
    a = gx + gp_ref[1:2, :]
    softplus = jnp.maximum(a, 0.0) + jnp.log(1.0 + jnp.exp(-jnp.abs(a)))
    gdec = -jnp.exp(gp_ref[0:1, :]) * softplus
    bg_ref[0] = jnp.where(lane < 8, _sigmoid(gx), jnp.where(lane < 16, gdec, 0.0))


def _gdn_prep(cqkv, gates, conv_w8, gparams, *, ctx_tiles):
    b, t, cch = cqkv.shape
    nt = t // TM
    r8 = TM // 8
    tile = lambda width: pl.BlockSpec((1, TM, width), lambda i, j: (i, j, 0))
    return pl.pallas_call(
        functools.partial(_gdn_prep_kernel, ctx_tiles=ctx_tiles, nt=nt),
        grid=(b, nt),
        in_specs=[tile(cch),
                  pl.BlockSpec((1, 8, cch), lambda i, j: (i, jnp.maximum(j * r8 - 1, 0), 0)),
                  pl.BlockSpec((1, 8, cch), lambda i, j: (i, jnp.minimum((j + 1) * r8, nt * r8 - 1), 0)),
                  tile(LANES),
                  pl.BlockSpec(conv_w8.shape, lambda i, j: (0, 0)),
                  pl.BlockSpec(gparams.shape, lambda i, j: (0, 0))],
        out_specs=(tile(512), tile(512), tile(512), tile(LANES)),
        out_shape=(jax.ShapeDtypeStruct((b, t, 512), BF16), jax.ShapeDtypeStruct((b, t, 512), BF16),
                   jax.ShapeDtypeStruct((b, t, 512), BF16), jax.ShapeDtypeStruct((b, t, LANES), F32)),
        compiler_params=_cparams(("parallel", "parallel")),
        name="gdn_prep",
    )(cqkv, cqkv, cqkv, gates, conv_w8, gparams)


def _unit_tri_inverses(mats):
    n = mats[0].shape[0]
    r = lax.broadcasted_iota(jnp.int32, (n, n), 0)
    c = lax.broadcasted_iota(jnp.int32, (n, n), 1)
    eye = jnp.where(r == c, 1.0, 0.0)
    ps = [-a for a in mats]
    ts = [eye + p for p in ps]
    for _ in range(5):
        ps = [_mm3(p, p) for p in ps]
        ts = [t + _mm3(t, p) for t, p in zip(ts, ps)]
    return ts


def _gdn_chunk_kernel(q_ref, k_ref, v_ref, bg_ref, l1_ref, l2_ref, u_ref, gl_ref):
    L = GDN_CHUNK
    r = lax.broadcasted_iota(jnp.int32, (L, L), 0)
    c = lax.broadcasted_iota(jnp.int32, (L, L), 1)
    chains = [(n, d, h) for n in range(GDN_CHUNKS_PER_STEP) for d in range(2) for h in range(GDN_HEADS)]
    incl = [(r >= c), (r <= c)]
    strict = [(r > c), (r < c)]
    rows = [slice(n * L, (n + 1) * L) for n in range(GDN_CHUNKS_PER_STEP)]
    bgs = [bg_ref[0, rw, :] for rw in rows]
    gcs = [[_mm_exact_lhs(jnp.where(m, 1.0, 0.0).astype(BF16), bg) for m in incl] for bg in bgs]
    gcts = [[g.T for g in gc] for gc in gcs]
    sl = [slice(h * GDN_DK, (h + 1) * GDN_DK) for h in range(GDN_HEADS)]
    qs = [[q_ref[0, rw, s] for s in sl] for rw in rows]
    ks = [[k_ref[0, rw, s] for s in sl] for rw in rows]
    kfs = [[k.astype(F32) for k in kk] for kk in ks]
    gcc = [gcs[n][d][:, 8 + 4 * d + h:9 + 4 * d + h] for n, d, h in chains]
    gcr = [gcts[n][d][8 + 4 * d + h:9 + 4 * d + h, :] for n, d, h in chains]
    dec = [jnp.where(incl[d], jnp.exp(jnp.where(incl[d], gcc[i] - gcr[i], 0.0)), 0.0) for i, (n, d, h) in enumerate(chains)]
    beta = [bgs[n][:, 4 * d + h:4 * d + h + 1] for n, d, h in chains]
    kbeta = [kfs[n][h] * beta[i] for i, (n, d, h) in enumerate(chains)]
    amat = [jnp.where(strict[d], _mm_nt(kbeta[i], ks[n][h]) * dec[i], 0.0) for i, (n, d, h) in enumerate(chains)]
    tinv = _unit_tri_inverses(amat)
    eg = [jnp.exp(g) for g in gcc]
    rhs = [jnp.concatenate([v_ref[0, rows[n], sl[h]].astype(F32) * beta[i], kbeta[i] * eg[i]], axis=1)
           for i, (n, d, h) in enumerate(chains)]
    sol = [_mm3(t, x) for t, x in zip(tinv, rhs)]
    qk = [[_mm_nt(qs[n][h], ks[n][h]) for h in range(GDN_HEADS)] for n in range(GDN_CHUNKS_PER_STEP)]
    for i, (n, d, h) in enumerate(chains):
        last = L - 1 if d == 0 else 0
        glast = gcc[i][last:last + 1, :]
        kdec = kfs[n][h] * jnp.exp(glast - gcc[i])
        u_ref[0, d, n, h] = sol[i][:, :GDN_DK]
        l1_ref[0, d, n, h, 0:L, :] = sol[i][:, GDN_DK:].astype(BF16)
        l1_ref[0, d, n, h, L:2 * L, :] = (qs[n][h].astype(F32) * eg[i]).astype(BF16)
        l2_ref[0, d, n, h, 0:L, :] = (qk[n][h] * dec[i]).astype(BF16)
        l2_ref[0, d, n, h, L:, :] = kdec.T.astype(BF16)
        gl_ref[0, d, n, h:h + 1, :] = jnp.broadcast_to(jnp.exp(glast), (1, LANES))
    for d in range(2):
        for n in range(GDN_CHUNKS_PER_STEP):
            gl_ref[0, d, n, GDN_HEADS:, :] = jnp.zeros((8 - GDN_HEADS, LANES), F32)


GDN_CHUNKS_PER_STEP = 2


def _gdn_chunks(qg, kg, vg, bg):
    b, t, _ = qg.shape
    nc = t // GDN_CHUNK
    L = GDN_CHUNK
    cps = GDN_CHUNKS_PER_STEP
    assert nc % cps == 0
    tile = lambda width: pl.BlockSpec((1, cps * L, width), lambda i, j: (i, j, 0))
    return pl.pallas_call(
        _gdn_chunk_kernel,
        grid=(b, nc // cps),
        in_specs=[tile(512), tile(512), tile(512), tile(LANES)],
        out_specs=(pl.BlockSpec((1, 2, cps, 4, 2 * L, GDN_DK), lambda i, j: (i, 0, j, 0, 0, 0)),
                   pl.BlockSpec((1, 2, cps, 4, L + GDN_DK, L), lambda i, j: (i, 0, j, 0, 0, 0)),
                   pl.BlockSpec((1, 2, cps, 4, L, GDN_DK), lambda i, j: (i, 0, j, 0, 0, 0)),
                   pl.BlockSpec((1, 2, cps, 8, LANES), lambda i, j: (i, 0, j, 0, 0))),
        out_shape=(jax.ShapeDtypeStruct((b, 2, nc, 4, 2 * L, GDN_DK), BF16),
                   jax.ShapeDtypeStruct((b, 2, nc, 4, L + GDN_DK, L), BF16),
                   jax.ShapeDtypeStruct((b, 2, nc, 4, L, GDN_DK), F32),
                   jax.ShapeDtypeStruct((b, 2, nc, 8, LANES), F32)),
        compiler_params=_cparams(("parallel", "parallel")),
        name="gdn_chunks",
    )(qg, kg, vg, bg)


def _gdn_scan_kernel(l1f_ref, l2f_ref, uf_ref, glf_ref, l1b_ref, l2b_ref, ub_ref, glb_ref,
                     of_ref, ob_ref, s_ref):
    L = GDN_CHUNK

    @pl.when(pl.program_id(1) == 0)
    def _():
        s_ref[...] = jnp.zeros_like(s_ref)

    dirs = ((l1f_ref, l2f_ref, uf_ref, glf_ref, of_ref), (l1b_ref, l2b_ref, ub_ref, glb_ref, ob_ref))
    chains = [(d, h) for d in range(2) for h in range(GDN_HEADS)]
    ss = [s_ref[i] for i in range(len(chains))]
    r1 = [_mm(dirs[d][0][0, 0, 0, h], ss[i]) for i, (d, h) in enumerate(chains)]
    v_new = [dirs[d][2][0, 0, 0, h] - r1[i][:L] for i, (d, h) in enumerate(chains)]
    r2 = [_mm(dirs[d][1][0, 0, 0, h], v_new[i]) for i, (d, h) in enumerate(chains)]
    for i, (d, h) in enumerate(chains):
        dirs[d][4][0, :, h * GDN_DK:(h + 1) * GDN_DK] = r1[i][L:] + r2[i][:L]
        s_ref[i] = ss[i] * dirs[d][3][0, 0, 0, h:h + 1, :] + r2[i][L:]


def _gdn_scan(l1, l2, u, gl, *, ctx_chunks):
    b, _, nc = l1.shape[:3]
    L = GDN_CHUNK

    def rev(j):
        return jnp.where(j < ctx_chunks, ctx_chunks - 1 - j, nc - 1 - (j - ctx_chunks))

    def spec(a, d):
        blk = (1, 1, 1) + a.shape[3:]
        zeros = (0,) * (a.ndim - 3)
        if d == 0:
            return pl.BlockSpec(blk, lambda i, j: (i, 0, j) + zeros)
        return pl.BlockSpec(blk, lambda i, j: (i, 1, rev(j)) + zeros)

    return pl.pallas_call(
        _gdn_scan_kernel,
        grid=(b, nc),
        in_specs=[spec(l1, 0), spec(l2, 0), spec(u, 0), spec(gl, 0),
                  spec(l1, 1), spec(l2, 1), spec(u, 1), spec(gl, 1)],
        out_specs=(pl.BlockSpec((1, L, 512), lambda i, j: (i, j, 0)),
                   pl.BlockSpec((1, L, 512), lambda i, j: (i, rev(j), 0))),
        out_shape=(jax.ShapeDtypeStruct((b, nc * L, 512), F32), jax.ShapeDtypeStruct((b, nc * L, 512), F32)),
        scratch_shapes=[pltpu.VMEM((2 * GDN_HEADS, GDN_DK, GDN_DK), F32)],
        compiler_params=_cparams(("parallel", "arbitrary")),
        name="gdn_scan",
    )(l1, l2, u, gl, l1, l2, u, gl)


def _merge_kernel(x_ref, h_ref, yac_ref, yal_ref, ybc_ref, ybl_ref, of_ref, ob_ref, z_ref, mod_ref, gnw_ref, n2w_ref,
                  wg_ref, wb_ref, wo_ref, wr_ref, xo_ref, h2_ref, aff_ref, afft_ref, *, nb, ctx_tiles, d):
    m = _mod_row(mod_ref, nb, ctx_tiles)
    g1, sh2, sc2 = m[:, 2 * d:3 * d], m[:, 3 * d:4 * d], m[:, 4 * d:5 * d]
    o = of_ref[0] + ob_ref[0]
    z = z_ref[0].astype(F32)
    ygs = []
    for h in range(GDN_HEADS):
        sl = slice(h * GDN_DK, (h + 1) * GDN_DK)
        ygs.append(_rms(o[:, sl], gnw_ref[...]) * _silu(z[:, sl]))
    yg = jnp.concatenate(ygs, axis=1).astype(BF16)
    hb = h_ref[0]
    merged = jnp.zeros((TM, d), F32)
    is_ctx = pl.program_id(1) < ctx_tiles
    ya = jnp.where(is_ctx, yac_ref[0], yal_ref[0])
    yb = jnp.where(is_ctx, ybc_ref[0], ybl_ref[0])
    for i, y in enumerate((ya, yb, yg)):
        gate = _sigmoid(jnp.dot(hb, wg_ref[i], preferred_element_type=F32))
        merged = merged + gate * jnp.dot(y, wb_ref[i], preferred_element_type=F32)
    x = x_ref[0] + g1 * jnp.dot(merged.astype(BF16), wo_ref[...], preferred_element_type=F32)
    xo_ref[0] = x
    h2 = _rms(x, n2w_ref[...]) * (1.0 + sc2) + sh2
    h2b = h2.astype(BF16)
    h2_ref[0] = h2b
    logits = _mm3(h2, wr_ref[...])
    lane = lax.broadcasted_iota(jnp.int32, logits.shape, 1)
    logits = jnp.where(lane < N_EXPERTS, logits, -jnp.inf)
    e = jnp.exp(logits - jnp.max(logits, axis=-1, keepdims=True))
    aff = e / jnp.sum(e, axis=-1, keepdims=True)
    aff_ref[0] = aff
    afft_ref[0] = aff.T[:N_EXPERTS]


def _merge(xs, h, ya, yb, o_f, o_b, z, mod, gnw, n2w, wg, wb, wo, wr, *, nb, ctx_tiles):
    b, t, d = xs.shape
    nt = t // TM
    tile = lambda width: pl.BlockSpec((1, TM, width), lambda i, j: (i, j, 0))
    whole = lambda a: pl.BlockSpec(a.shape, lambda i, j: (0,) * a.ndim)
    (yac, yal), (ybc, ybl) = ya, yb
    ctx_tile = pl.BlockSpec((1, TM, 512), lambda i, j: (i, jnp.minimum(j, ctx_tiles - 1), 0))
    lat_tile = pl.BlockSpec((1, TM, 512), lambda i, j: (i, jnp.maximum(j - ctx_tiles, 0), 0))
    return pl.pallas_call(
        functools.partial(_merge_kernel, nb=nb, ctx_tiles=ctx_tiles, d=d),
        grid=(b, nt),
        in_specs=[tile(d), tile(d), ctx_tile, lat_tile, ctx_tile, lat_tile, tile(512), tile(512), tile(512),
                  whole(mod), whole(gnw), whole(n2w), whole(wg), whole(wb), whole(wo), whole(wr)],
        out_specs=(tile(d), tile(d), tile(LANES), pl.BlockSpec((1, N_EXPERTS, TM), lambda i, j: (i, 0, j))),
        out_shape=(jax.ShapeDtypeStruct((b, t, d), F32), jax.ShapeDtypeStruct((b, t, d), BF16),
                   jax.ShapeDtypeStruct((b, t, LANES), F32), jax.ShapeDtypeStruct((b, N_EXPERTS, t), F32)),
        compiler_params=_cparams(("parallel", "parallel")),
        name="merge",
    )(xs, h, yac, yal, ybc, ybl, o_f, o_b, z, mod, gnw, n2w, wg, wb, wo, wr)


ROW_ALIGN = 16
SLOT_GROUP = 64
FFN_CHUNKS = 3


def _moe_rows(caps, nt):
    chunk = -(-(sum(caps) + (ROW_ALIGN - 1) * nt) // (ROW_ALIGN * FFN_CHUNKS)) * ROW_ALIGN
    return chunk, FFN_CHUNKS * chunk + TM


def _route_kernel(afft_ref, post_ref, posc_ref, starts_ref, *, ctx_tiles, nt, cap_ctx, cap_lat):
    a = afft_ref[0]
    bits = pltpu.bitcast(a, jnp.int32)
    lane = lax.broadcasted_iota(jnp.int32, a.shape, 1)
    isctx = lane < ctx_tiles * TM

    def bisect(i, carry):
        tc, tl = carry
        bit = jnp.left_shift(jnp.int32(1), 30 - i)
        cc, cl = tc | bit, tl | bit
        ge = bits >= jnp.where(isctx, cc, cl)
        n_c = jnp.sum(jnp.where(jnp.logical_and(ge, isctx), 1.0, 0.0), axis=-1, keepdims=True)
        n_l = jnp.sum(jnp.where(jnp.logical_and(ge, jnp.logical_not(isctx)), 1.0, 0.0), axis=-1, keepdims=True)
        return jnp.where(n_c >= cap_ctx, cc, tc), jnp.where(n_l >= cap_lat, cl, tl)

    zero = jnp.zeros((N_EXPERTS, 1), jnp.int32)
    tc, tl = lax.fori_loop(0, 31, bisect, (zero, zero))
    thr = jnp.where(isctx, tc, tl)
    gt = bits > thr
    eq = bits == thr
    n_gt_c = jnp.sum(jnp.where(jnp.logical_and(gt, isctx), 1.0, 0.0), axis=-1, keepdims=True)
    n_gt_l = jnp.sum(jnp.where(jnp.logical_and(gt, jnp.logical_not(isctx)), 1.0, 0.0), axis=-1, keepdims=True)
    r = lax.broadcasted_iota(jnp.int32, (TM, TM), 0)
    c = lax.broadcasted_iota(jnp.int32, (TM, TM), 1)
    upper = jnp.where(r < c, 1.0, 0.0).astype(BF16)
    lane_t = lax.broadcasted_iota(jnp.int32, (N_EXPERTS, LANES), 1)
    starts = jnp.zeros((N_EXPERTS, LANES), F32)
    start = jnp.zeros((N_EXPERTS, 1), F32)
    eq_seen = jnp.zeros((N_EXPERTS, 1), F32)
    for j in range(nt):
        sl = slice(j * TM, (j + 1) * TM)
        if j == ctx_tiles:
            eq_seen = jnp.zeros((N_EXPERTS, 1), F32)
        need = (cap_ctx - n_gt_c) if j < ctx_tiles else (cap_lat - n_gt_l)
        eq_j = jnp.where(eq[:, sl], 1.0, 0.0)
        rank = eq_seen + _mm(eq_j, upper)
        eq_seen = eq_seen + jnp.sum(eq_j, axis=-1, keepdims=True)
        sel = jnp.logical_or(gt[:, sl], jnp.logical_and(eq[:, sl], rank < need))
        sel_f = jnp.where(sel, 1.0, 0.0)
        pos = jnp.where(sel, start + _mm(sel_f, upper), -1.0)
        post_ref[0, :, sl] = pos
        posc_ref[0, sl, :] = jnp.concatenate([pos, jnp.full((LANES - N_EXPERTS, TM), -1.0, F32)], axis=0).T
        starts = jnp.where(lane_t == j, start, starts)
        cnt = jnp.sum(sel_f, axis=-1, keepdims=True)
        start = start + jnp.floor((cnt + (ROW_ALIGN - 1)) * (1.0 / ROW_ALIGN)) * ROW_ALIGN
    starts_ref[0] = jnp.where(lane_t == nt, start, starts).astype(jnp.int32)


def _route(afft, *, ctx_tiles, cap_ctx, cap_lat):
    b, e, t = afft.shape
    nt = t // TM
    return pl.pallas_call(
        functools.partial(_route_kernel, ctx_tiles=ctx_tiles, nt=nt, cap_ctx=cap_ctx, cap_lat=cap_lat),
        grid=(b,),
        in_specs=[pl.BlockSpec((1, e, t), lambda i: (i, 0, 0))],
        out_specs=(pl.BlockSpec((1, e, t), lambda i: (i, 0, 0)),
                   pl.BlockSpec((1, t, LANES), lambda i: (i, 0, 0)),
                   pl.BlockSpec((1, e, LANES), lambda i: (i, 0, 0))),
        out_shape=(jax.ShapeDtypeStruct((b, e, t), F32), jax.ShapeDtypeStruct((b, t, LANES), F32),
                   jax.ShapeDtypeStruct((b, e, LANES), jnp.int32)),
        compiler_params=_cparams(("parallel",)),
        name="moe_route",
    )(afft)


def _moe_ffn_kernel(starts_ref, h2_ref, post_ref, wg_ref, wu_ref, wd_ref, y_ref, xin_ref, *, nt, unroll, chunk):
    b, e = pl.program_id(0), pl.program_id(1)
    xin_ref[...] = jnp.zeros_like(xin_ref)
    slot = lax.broadcasted_iota(jnp.int32, (SLOT_GROUP, TM), 0).astype(F32)

    def picks(j, g):
        base = (b * N_EXPERTS + e) * LANES + j
        off = starts_ref[base]
        tok = pl.multiple_of(j * TM, TM)
        pos = post_ref[0, pl.ds(e, 1), pl.ds(tok, TM)] - (off + g * SLOT_GROUP).astype(F32)
        onehot = jnp.where(pos == slot, 1.0, 0.0).astype(BF16)
        blk = jnp.dot(onehot, h2_ref[0, pl.ds(tok, TM), :], preferred_element_type=F32)
        return pl.multiple_of(off + g * SLOT_GROUP, ROW_ALIGN), starts_ref[base + 1] - off - g * SLOT_GROUP, blk

    def first_groups(i, carry):
        blocks = [picks(i * unroll + k, 0) for k in range(unroll)]
        for row0, _, blk in blocks:
            xin_ref[pl.ds(row0, SLOT_GROUP), :] = blk
        return carry

    lax.fori_loop(0, nt // unroll, first_groups, 0)

    def more_groups(j, carry):
        def group(g, carry):
            row0, n_real, blk = picks(j, g)
            keep = lax.broadcasted_iota(jnp.int32, blk.shape, 0) < n_real
            xin_ref[pl.ds(row0, SLOT_GROUP), :] = jnp.where(keep, blk, xin_ref[pl.ds(row0, SLOT_GROUP), :])
            return carry

        base = (b * N_EXPERTS + e) * LANES + j
        n_groups = (starts_ref[base + 1] - starts_ref[base] + (SLOT_GROUP - 1)) // SLOT_GROUP
        return lax.fori_loop(1, n_groups, group, carry)

    lax.fori_loop(0, nt, more_groups, 0)

    def ffn(c, carry):
        rows = pl.ds(pl.multiple_of(c * chunk, ROW_ALIGN), chunk)
        x = xin_ref[rows, :].astype(BF16)
        hid = _silu(jnp.dot(x, wg_ref[0], preferred_element_type=F32)) * jnp.dot(x, wu_ref[0], preferred_element_type=F32)
        y_ref[0, 0, rows, :] = jnp.dot(hid.astype(BF16), wd_ref[0], preferred_element_type=F32).astype(BF16)
        return carry

    lax.fori_loop(0, FFN_CHUNKS, ffn, 0)
    y_ref[0, 0, FFN_CHUNKS * chunk:, :] = jnp.zeros((TM, y_ref.shape[3]), BF16)


def _moe_ffn(starts, h2, post, wg, wu, wd, *, chunk, rows):
    b, t, d = h2.shape
    nt = t // TM
    e, _, ff = wg.shape
    grid_spec = pltpu.PrefetchScalarGridSpec(
        num_scalar_prefetch=1,
        grid=(b, e),
        in_specs=[pl.BlockSpec((1, t, d), lambda i, k, s: (i, 0, 0), pipeline_mode=pl.Buffered(1)),
                  pl.BlockSpec((1, e, t), lambda i, k, s: (i, 0, 0)),
                  pl.BlockSpec((1, d, ff), lambda i, k, s: (k, 0, 0)),
                  pl.BlockSpec((1, d, ff), lambda i, k, s: (k, 0, 0)),
                  pl.BlockSpec((1, ff, d), lambda i, k, s: (k, 0, 0))],
        out_specs=pl.BlockSpec((1, 1, rows, d), lambda i, k, s: (i, k, 0, 0)),
        scratch_shapes=[pltpu.VMEM((rows, d), F32)],
    )
    return pl.pallas_call(
        functools.partial(_moe_ffn_kernel, nt=nt, unroll=max(u for u in range(1, 12) if nt % u == 0), chunk=chunk),
        grid_spec=grid_spec,
        out_shape=jax.ShapeDtypeStruct((b, e, rows, d), BF16),
        compiler_params=pltpu.CompilerParams(dimension_semantics=("parallel", "arbitrary"),
                                             vmem_limit_bytes=MOE_FFN_VMEM_LIMIT),
        name="moe_ffn",
    )(starts, h2, post, wg, wu, wd)


def _moe_combine_kernel(starts_ref, y_ref, posc_ref, aff_ref, x_ref, g2_ref, o_ref, *, nb, ctx_tiles):
    b, j = pl.program_id(0), pl.program_id(2)
    slot = lax.broadcasted_iota(jnp.int32, (TM, TM), 1).astype(F32)
    posc, aff = posc_ref[0], aff_ref[0]
    acc = jnp.zeros(o_ref.shape[1:], F32)
    for e in range(N_EXPERTS):
        off = pl.multiple_of(starts_ref[(b * N_EXPERTS + e) * LANES + j], ROW_ALIGN)
        pos = posc[:, e:e + 1]
        onehot = jnp.where(pos - off.astype(F32) == slot, 1.0, 0.0).astype(BF16)
        w = jnp.where(pos >= 0.0, aff[:, e:e + 1], 0.0)
        acc = acc + w * jnp.dot(onehot, y_ref[0, e, pl.ds(off, TM), :], preferred_element_type=F32)
    row = jnp.where(j < ctx_tiles, nb, b)
    o_ref[0] = x_ref[0] + g2_ref[pl.ds(row, 1), :] * acc


def _moe_combine(starts, y, posc, aff, xs, g2, *, nb, ctx_tiles, dsplit=2):
    b, t, d = xs.shape
    e, rows = y.shape[1:3]
    dq = d // dsplit
    grid_spec = pltpu.PrefetchScalarGridSpec(
        num_scalar_prefetch=1,
        grid=(b, dsplit, t // TM),
        in_specs=[pl.BlockSpec((1, e, rows, dq), lambda i, q, j, s: (i, 0, 0, q), pipeline_mode=pl.Buffered(1)),
                  pl.BlockSpec((1, TM, LANES), lambda i, q, j, s: (i, j, 0)),
                  pl.BlockSpec((1, TM, LANES), lambda i, q, j, s: (i, j, 0)),
                  pl.BlockSpec((1, TM, dq), lambda i, q, j, s: (i, j, q)),
                  pl.BlockSpec((MOD_ROWS, dq), lambda i, q, j, s: (0, q))],
        out_specs=pl.BlockSpec((1, TM, dq), lambda i, q, j, s: (i, j, q)),
    )
    return pl.pallas_call(
        functools.partial(_moe_combine_kernel, nb=nb, ctx_tiles=ctx_tiles),
        grid_spec=grid_spec,
        out_shape=jax.ShapeDtypeStruct((b, t, d), F32),
        compiler_params=_cparams(("parallel", "parallel", "parallel")),
        name="moe_combine",
    )(starts, y, posc, aff, xs, g2)


def _lambda_init(layer_idx):
    return 0.8 - 0.6 * math.exp(-0.3 * layer_idx)


def _rope_tables(n_lat, n_ctx):
    rows = n_lat // GRID_W
    row = jnp.repeat(jnp.arange(rows, dtype=F32), GRID_W)
    col = jnp.tile(jnp.arange(GRID_W, dtype=F32), rows)
    axis_dim = HEAD_DIM // 2
    inv_freq = ROPE_THETA ** (-jnp.arange(0, axis_dim, 2, dtype=F32) / axis_dim)
    ang_r, ang_c = row[:, None] * inv_freq, col[:, None] * inv_freq
    cos = jnp.concatenate([jnp.cos(ang_r)] * 2 + [jnp.cos(ang_c)] * 2, axis=-1)
    sin = jnp.concatenate([-jnp.sin(ang_r), jnp.sin(ang_r), -jnp.sin(ang_c), jnp.sin(ang_c)], axis=-1)
    cos = jnp.concatenate([jnp.ones((n_ctx, HEAD_DIM), F32), cos], axis=0)
    sin = jnp.concatenate([jnp.zeros((n_ctx, HEAD_DIM), F32), sin], axis=0)
    return jnp.tile(cos, (1, 2)), jnp.tile(sin, (1, 2))


def kernel(x, c, ctx, c_ctx, w_mod, b_mod, norm1_w, norm2_w, w_in, gqa_q_norm, gqa_k_norm, diff_q_norm,
           diff_k_norm, diff_lambda_q1, diff_lambda_k1, diff_lambda_q2, diff_lambda_k2, diff_subln,
           gdn_conv_w, gdn_a_log, gdn_dt_bias, gdn_norm_w, w_merge_gate, w_branch, w_out, w_router,
           w_exp_gate, w_exp_up, w_exp_down):
    nb, n_lat, d = x.shape
    n_ctx = ctx.shape[1]
    depth = w_mod.shape[0]
    assert n_ctx % TM == 0 and n_lat % TM == 0 and nb < MOD_ROWS and d == 1024
    ctx_tiles = n_ctx // TM
    ctx_chunks = n_ctx // GDN_CHUNK

    cos_t, sin_t = _rope_tables(n_lat, n_ctx)
    cc = jnp.zeros((MOD_ROWS, d), F32).at[:nb].set(c).at[nb].set(c_ctx)
    mods = _modulation(cc, w_mod, b_mod)
    half = lax.broadcasted_iota(jnp.int32, (LANES, LANES), 0) // HEAD_DIM
    bd = (half == half.T).astype(BF16)

    xs = jnp.concatenate([ctx, x], axis=1)
    scale = HEAD_DIM ** -0.5
    for li in range(depth):
        lam_init = _lambda_init(li)
        w_in_p = jnp.pad(w_in[li], ((0, 0), (0, 4480 - w_in.shape[-1]))).astype(BF16)
        qkn = jnp.stack([jnp.tile(gqa_q_norm[li] * scale, 2), jnp.tile(gqa_k_norm[li], 2),
                         jnp.tile(diff_q_norm[li] * scale, 2), jnp.tile(diff_k_norm[li], 2)])
        (h, qa, ka, va, qb, kb, vb, cqkv, z, gates) = _inproj(
            xs, mods[li], norm1_w[li][None, :], w_in_p, cos_t, sin_t, qkn, bd, nb=nb, ctx_tiles=ctx_tiles)

        ya = _gqa_attention(qa, ka, va, n_ctx=n_ctx)
        lam_rows = jnp.stack([jnp.concatenate([diff_lambda_q1[li], diff_lambda_q2[li]]),
                              jnp.concatenate([diff_lambda_k1[li], diff_lambda_k2[li]])])
        yb = _diff_attention(qb, kb, vb, lam_rows, diff_subln[li][None, :], n_ctx=n_ctx, lam_init=lam_init)

        conv_w8 = jnp.pad(gdn_conv_w[li], ((0, 5), (0, 0)))
        gparams = jnp.zeros((2, LANES), F32)
        gparams = gparams.at[0, 8:16].set(gdn_a_log[li].reshape(-1)).at[1, 8:16].set(gdn_dt_bias[li].reshape(-1))
        qg, kg, vg, bg = _gdn_prep(cqkv, gates, conv_w8, gparams, ctx_tiles=ctx_tiles)
        l1, l2, u, gl = _gdn_chunks(qg, kg, vg, bg)
        o_f, o_b = _gdn_scan(l1, l2, u, gl, ctx_chunks=ctx_chunks)

        wr = jnp.pad(w_router[li], ((0, 0), (0, LANES - N_EXPERTS)))
        xs, h2, aff, afft = _merge(xs, h, ya, yb, o_f, o_b, z, mods[li], gdn_norm_w[li][None, :],
                                   norm2_w[li][None, :], w_merge_gate[li].astype(BF16), w_branch[li].astype(BF16),
                                   w_out[li].astype(BF16), wr, nb=nb, ctx_tiles=ctx_tiles)

        caps = (CAPACITY_FACTOR * n_ctx // N_EXPERTS, CAPACITY_FACTOR * n_lat // N_EXPERTS)
        chunk, rows = _moe_rows(caps, (n_ctx + n_lat) // TM)
        post, posc, starts = _route(afft, ctx_tiles=ctx_tiles, cap_ctx=caps[0], cap_lat=caps[1])
        starts = starts.reshape(-1)
        y = _moe_ffn(starts, h2, post, w_exp_gate[li].astype(BF16), w_exp_up[li].astype(BF16),
                     w_exp_down[li].astype(BF16), chunk=chunk, rows=rows)
        xs = _moe_combine(starts, y, posc, aff, xs, mods[li, :, 5 * d:], nb=nb, ctx_tiles=ctx_tiles)
    return xs[:, n_ctx:]
```

```python
import functools
import math

import jax
import jax.numpy as jnp
from jax import lax
from jax.experimental import pallas as pl
from jax.experimental.pallas import tpu as pltpu

F32 = jnp.float32
BF16 = jnp.bfloat16

LANES = 128
TM = 256
GRID_W = 64
HEAD_DIM = 64
GQA_HEADS = 8
GQA_KV_HEADS = 2
DIFF_HEADS = 4
GDN_HEADS = 4
GDN_DK = 128
GDN_CHUNK = 64
N_EXPERTS = 16
CAPACITY_FACTOR = 2
ROPE_THETA = 10000.0
EPS = 1e-6
ONES_ROWS = 16
MOD_ROWS = 16
VMEM_LIMIT = 56 * 1024 * 1024
MOE_FFN_VMEM_LIMIT = 60 * 1024 * 1024


def _cparams(sem):
    return pltpu.CompilerParams(dimension_semantics=sem, vmem_limit_bytes=VMEM_LIMIT)


def _mm(a, b):
    return jnp.dot(a.astype(BF16), b.astype(BF16), preferred_element_type=F32)


def _mm_nt(a, b):
    return lax.dot_general(a.astype(BF16), b.astype(BF16), (((1,), (1,)), ((), ())),
                           preferred_element_type=F32)


def _split2(a):
    hi = a.astype(BF16)
    lo = (a - hi.astype(F32)).astype(BF16)
    return hi, lo


def _mm3(a, b):
    ah, al = _split2(a)
    bh, bl = _split2(b)
    return _mm(ah, bh) + _mm(ah, bl) + _mm(al, bh)


def _mm_exact_lhs(a_bf16, b):
    b1 = b.astype(BF16)
    r1 = b - b1.astype(F32)
    b2 = r1.astype(BF16)
    b3 = (r1 - b2.astype(F32)).astype(BF16)
    return _mm(a_bf16, b1) + _mm(a_bf16, b2) + _mm(a_bf16, b3)


def _sigmoid(x):
    return 1.0 / (1.0 + jnp.exp(-x))


def _silu(x):
    return x * _sigmoid(x)


def _mod_kernel(a_ref, w_ref, b_ref, o_ref):
    a = _silu(a_ref[...])
    o_ref[0] = _mm3(a, w_ref[0]) + b_ref[0]


def _modulation(cc, w_mod, b_mod):
    depth, d, d6 = w_mod.shape
    nj = d6 // d
    return pl.pallas_call(
        _mod_kernel,
        grid=(depth, nj),
        in_specs=[pl.BlockSpec((MOD_ROWS, d), lambda l, j: (0, 0)),
                  pl.BlockSpec((1, d, d), lambda l, j: (l, 0, j)),
                  pl.BlockSpec((1, 1, d), lambda l, j: (l, 0, j))],
        out_specs=pl.BlockSpec((1, MOD_ROWS, d), lambda l, j: (l, 0, j)),
        out_shape=jax.ShapeDtypeStruct((depth, MOD_ROWS, d6), F32),
        compiler_params=_cparams(("parallel", "parallel")),
        name="modulation",
    )(cc, w_mod, b_mod.reshape(depth, 1, d6))


def _mod_row(mod_ref, nb, ctx_tiles):
    b = pl.program_id(0)
    t = pl.program_id(1)
    row = jnp.where(t < ctx_tiles, nb, b)
    return mod_ref[pl.ds(row, 1), :]


def _rms(x, w):
    return x * lax.rsqrt(jnp.mean(x * x, axis=-1, keepdims=True) + EPS) * w


def _half_norm_rope(t, w, bd, cos, sin):
    ssq = _mm(t * t, bd)
    tn = t * lax.rsqrt(ssq * (1.0 / HEAD_DIM) + EPS) * w
    lane = lax.broadcasted_iota(jnp.int32, tn.shape, 1)
    partner = jnp.where((lane % 32) < 16, pltpu.roll(tn, LANES - 16, axis=1), pltpu.roll(tn, 16, axis=1))
    return tn * cos + partner * sin


def _inproj_kernel(x_ref, mod_ref, nw_ref, w_ref, cos_ref, sin_ref, qkn_ref, bd_ref,
                   h_ref, qa_ref, ka_ref, va_ref, qb_ref, kb_ref, vb_ref, c_ref, z_ref, g_ref,
                   *, nb, ctx_tiles, d):
    m = _mod_row(mod_ref, nb, ctx_tiles)
    sh, sc = m[:, 0:d], m[:, d:2 * d]
    h = _rms(x_ref[0], nw_ref[...]) * (1.0 + sc) + sh
    hb = h.astype(BF16)
    h_ref[0] = hb
    cos, sin, bd = cos_ref[...], sin_ref[...], bd_ref[...]

    def proj(lo, width):
        return jnp.dot(hb, w_ref[:, lo:lo + width], preferred_element_type=F32)

    ones_rows = jnp.ones((ONES_ROWS, TM), BF16)
    for g in range(4):
        t = _half_norm_rope(proj(g * LANES, LANES), qkn_ref[0:1, :], bd, cos, sin)
        qa_ref[0, g * LANES:(g + 1) * LANES, :] = t.T.astype(BF16)
    ka_ref[0] = _half_norm_rope(proj(512, LANES), qkn_ref[1:2, :], bd, cos, sin).astype(BF16)
    vt = proj(640, LANES).T.astype(BF16)
    for j in range(GQA_KV_HEADS):
        va_ref[0, j, 0:HEAD_DIM, :] = vt[j * HEAD_DIM:(j + 1) * HEAD_DIM]
        va_ref[0, j, HEAD_DIM:HEAD_DIM + ONES_ROWS, :] = ones_rows
    for g in range(4):
        t = _half_norm_rope(proj(768 + g * LANES, LANES), qkn_ref[2:3, :], bd, cos, sin)
        qb_ref[0, g * LANES:(g + 1) * LANES, :] = t.T.astype(BF16)
        t = proj(1280 + g * LANES, LANES)
        kb_ref[0, :, g * LANES:(g + 1) * LANES] = _half_norm_rope(t, qkn_ref[3:4, :], bd, cos, sin).astype(BF16)
        vb_ref[0, g, 0:LANES, :] = proj(1792 + g * LANES, LANES).T.astype(BF16)
        vb_ref[0, g, LANES:LANES + ONES_ROWS, :] = ones_rows
    for g in range(3):
        c_ref[0, :, g * 512:(g + 1) * 512] = proj(2304 + g * 512, 512)
    z_ref[0] = proj(3840, 512).astype(BF16)
    g_ref[0] = proj(4352, LANES)


def _inproj(xs, mod, nw, w_in_p, cos_t, sin_t, qkn, bd, *, nb, ctx_tiles):
    b, t, d = xs.shape
    nt = t // TM
    tile = lambda width: pl.BlockSpec((1, TM, width), lambda i, j: (i, j, 0))
    whole = lambda a: pl.BlockSpec(a.shape, lambda i, j: (0,) * a.ndim)
    out_shape = (
        jax.ShapeDtypeStruct((b, t, d), BF16),
        jax.ShapeDtypeStruct((b, 512, t), BF16),
        jax.ShapeDtypeStruct((b, t, LANES), BF16),
        jax.ShapeDtypeStruct((b, 2, HEAD_DIM + ONES_ROWS, t), BF16),
        jax.ShapeDtypeStruct((b, 512, t), BF16),
        jax.ShapeDtypeStruct((b, t, 512), BF16),
        jax.ShapeDtypeStruct((b, 4, LANES + ONES_ROWS, t), BF16),
        jax.ShapeDtypeStruct((b, t, 1536), F32),
        jax.ShapeDtypeStruct((b, t, 512), BF16),
        jax.ShapeDtypeStruct((b, t, LANES), F32),
    )
    out_specs = (
        tile(d),
        pl.BlockSpec((1, 512, TM), lambda i, j: (i, 0, j)),
        tile(LANES),
        pl.BlockSpec((1, 2, HEAD_DIM + ONES_ROWS, TM), lambda i, j: (i, 0, 0, j)),
        pl.BlockSpec((1, 512, TM), lambda i, j: (i, 0, j)),
        tile(512),
        pl.BlockSpec((1, 4, LANES + ONES_ROWS, TM), lambda i, j: (i, 0, 0, j)),
        tile(1536), tile(512), tile(LANES),
    )
    return pl.pallas_call(
        functools.partial(_inproj_kernel, nb=nb, ctx_tiles=ctx_tiles, d=d),
        grid=(b, nt),
        in_specs=[tile(d), whole(mod), whole(nw), whole(w_in_p),
                  pl.BlockSpec((TM, LANES), lambda i, j: (j, 0)),
                  pl.BlockSpec((TM, LANES), lambda i, j: (j, 0)),
                  whole(qkn), whole(bd)],
        out_specs=out_specs,
        out_shape=out_shape,
        compiler_params=_cparams(("parallel", "parallel")),
        name="inproj",
    )(xs, mod, nw, w_in_p, cos_t, sin_t, qkn, bd)


def _softmax_step(st, vt, mx, acc):
    mnew = jnp.maximum(mx, jnp.max(st, axis=0, keepdims=True))
    p = jnp.exp2(st - mnew).astype(BF16)
    acc = acc * jnp.exp2(mx - mnew) + jnp.dot(vt, p, preferred_element_type=F32)
    return mnew, acc


def _attend_bounded(w, k_at, vt_at, n_ctx, n_lat, vrows):
    def step(off, n, acc):
        p = jnp.exp2(jnp.dot(k_at(off, n), w, preferred_element_type=F32)).astype(BF16)
        return acc + jnp.dot(vt_at(off, n), p, preferred_element_type=F32)

    acc = step(0, n_ctx, jnp.zeros((vrows, w.shape[1]), F32))
    tk = min(BOUNDED_TK, n_lat) if n_lat else 0
    for c in range(n_lat // tk if tk else 0):
        acc = step(n_ctx + c * tk, tk, acc)
    return acc


def _attend(w, k_at, vt_at, n_ctx, n_groups, tk, s_refs, vrows):
    unroll = len(s_refs)

    def scores(c):
        return jnp.dot(k_at(lat(c), tk), w, preferred_element_type=F32)

    def lat(c):
        return pl.multiple_of(n_ctx + c * tk, TM)

    mx = jnp.full((1, w.shape[1]), -jnp.inf, F32)
    acc = jnp.zeros((vrows, w.shape[1]), F32)
    mx, acc = _softmax_step(jnp.dot(k_at(0, n_ctx), w, preferred_element_type=F32), vt_at(0, n_ctx), mx, acc)
    if n_groups == 0:
        return acc

    def group(i, carry, prefetch):
        mx, acc = carry
        for u in range(unroll):
            c = i * unroll + u
            if u + 1 < unroll or prefetch:
                s_refs[(u + 1) % unroll][...] = scores(c + 1)
            mx, acc = _softmax_step(s_refs[u][...], vt_at(lat(c), tk), mx, acc)
        return mx, acc

    s_refs[0][...] = scores(0)
    carry = lax.fori_loop(0, n_groups - 1, functools.partial(group, prefetch=True), (mx, acc))
    return group(n_groups - 1, carry, False)[1]


SCORE_LIMIT = 64.0
BOUNDED_TK = 1024
ATTN_Q_BLOCKS = 2
ATTN_UNROLL = 4
ATTN_TK = 512


def _attn_chunk(n_lat):
    tk = min(ATTN_TK, max(TM, n_lat // ATTN_UNROLL))
    unroll = min(ATTN_UNROLL, n_lat // tk)
    assert n_lat % (unroll * tk) == 0 and tk % TM == 0 and n_lat % (ATTN_Q_BLOCKS * TM) == 0
    return tk, unroll, n_lat // (unroll * tk)


def _attention_call(kern, name, q_t, others, other_specs, *, n_ctx, heads, bounded):
    b, _, t = q_t.shape
    n_lat = t - n_ctx
    tk, unroll, n_groups = _attn_chunk(n_lat)
    nq, ctx_tiles = ATTN_Q_BLOCKS, n_ctx // TM
    tq = nq * TM
    q_specs = [pl.BlockSpec((1, LANES, TM), functools.partial(lambda i, g, j, u: (i, g, ctx_tiles + nq * j + u), u=u))
               for u in range(nq)]
    lat = pl.pallas_call(
        functools.partial(kern, nq=nq, n_ctx=n_ctx, n_lat=n_lat, n_groups=n_groups, tk=tk, bounded=bounded),
        grid=(b, heads, n_lat // tq),
        in_specs=q_specs + other_specs(t),
        out_specs=pl.BlockSpec((1, tq, LANES), lambda i, g, j: (i, j, g)),
        out_shape=jax.ShapeDtypeStruct((b, n_lat, 512), BF16),
        scratch_shapes=[] if bounded else [pltpu.VMEM((tk, 2 * tq), F32)] * unroll,
        compiler_params=_cparams(("parallel", "parallel", "parallel")),
        name=name,
    )(*([q_t] * nq), *others)
    ctx = pl.pallas_call(
        functools.partial(kern, nq=1, n_ctx=n_ctx, n_lat=0, n_groups=0, tk=tk, bounded=bounded),
        grid=(b, heads, ctx_tiles),
        in_specs=[pl.BlockSpec((1, LANES, TM), lambda i, g, j: (i, g, j))] + other_specs(n_ctx),
        out_specs=pl.BlockSpec((1, TM, LANES), lambda i, g, j: (i, j, g)),
        out_shape=jax.ShapeDtypeStruct((b, n_ctx, 512), BF16),
        compiler_params=_cparams(("parallel", "parallel", "parallel")),
        name=name + "_ctx",
    )(q_t, *others)
    return ctx, lat


def _gqa_kernel(*refs, nq, n_ctx, n_lat, n_groups, tk, bounded):
    qt_refs, (k_ref, vt_ref, o_ref), s_refs = refs[:nq], refs[nq:nq + 3], refs[nq + 3:]
    tq = nq * TM
    top = jnp.concatenate([r[0, :HEAD_DIM] for r in qt_refs] + [r[0, HEAD_DIM:] for r in qt_refs], axis=1)
    zero = jnp.zeros_like(top)
    w = jnp.where(pl.program_id(1) < 2, jnp.concatenate([top, zero], axis=0), jnp.concatenate([zero, top], axis=0))
    vrows = HEAD_DIM + ONES_ROWS
    k_at, vt_at = (lambda off, n: k_ref[0, pl.ds(off, n), :]), (lambda off, n: vt_ref[0, 0, :, pl.ds(off, n)])
    if bounded:
        acc = _attend_bounded(w, k_at, vt_at, n_ctx, n_lat, vrows)
    else:
        acc = _attend(w, k_at, vt_at, n_ctx, n_groups, tk, s_refs, vrows)
    o = acc[:HEAD_DIM] / acc[HEAD_DIM:HEAD_DIM + 1]
    o_ref[0] = jnp.concatenate([o[:, :tq], o[:, tq:]], axis=0).T.astype(BF16)


def _gqa_attention(qat, ka, vat, *, n_ctx, bounded):
    vrows = vat.shape[2]
    specs = lambda keys: [pl.BlockSpec((1, keys, LANES), lambda i, g, j: (i, 0, 0)),
                          pl.BlockSpec((1, 1, vrows, keys), lambda i, g, j: (i, g // 2, 0, 0))]
    return _attention_call(_gqa_kernel, "gqa_attention", qat, (ka, vat), specs, n_ctx=n_ctx, heads=4, bounded=bounded)


def _diff_kernel(*refs, nq, n_ctx, n_lat, n_groups, tk, bounded, lam_init):
    qt_refs, (k_ref, vt_ref, lam_ref, sw_ref, o_ref), s_refs = refs[:nq], refs[nq:nq + 5], refs[nq + 5:]
    tq = nq * TM
    qt = jnp.concatenate([r[0] for r in qt_refs], axis=1)
    zero = jnp.zeros((HEAD_DIM, tq), BF16)
    w = jnp.concatenate([jnp.concatenate([qt[:HEAD_DIM], zero], axis=0),
                         jnp.concatenate([zero, qt[HEAD_DIM:]], axis=0)], axis=1)
    vrows = LANES + ONES_ROWS
    k_at, vt_at = (lambda off, n: k_ref[0, pl.ds(off, n), :]), (lambda off, n: vt_ref[0, 0, :, pl.ds(off, n)])
    if bounded:
        acc = _attend_bounded(w, k_at, vt_at, n_ctx, n_lat, vrows)
    else:
        acc = _attend(w, k_at, vt_at, n_ctx, n_groups, tk, s_refs, vrows)
    prod = lam_ref[0:1, :] * lam_ref[1:2, :]
    low = lax.broadcasted_iota(jnp.int32, prod.shape, 1) < HEAD_DIM
    l1 = jnp.sum(jnp.where(low, prod, 0.0), axis=-1, keepdims=True)
    l2 = jnp.sum(jnp.where(low, 0.0, prod), axis=-1, keepdims=True)
    lam = jnp.exp(l1) - jnp.exp(l2) + lam_init
    o1 = acc[:LANES, :tq] / acc[LANES:LANES + 1, :tq]
    o2 = acc[:LANES, tq:] / acc[LANES:LANES + 1, tq:]
    o = (o1 - lam * o2).T
    o_ref[0] = (_rms(o, sw_ref[...]) * (1.0 - lam_init)).astype(BF16)


def _diff_attention(qbt, kb, vbt, lam_rows, subln, *, n_ctx, lam_init, bounded):
    vrows = vbt.shape[2]
    specs = lambda keys: [pl.BlockSpec((1, keys, LANES), lambda i, g, j: (i, 0, g)),
                          pl.BlockSpec((1, 1, vrows, keys), lambda i, g, j: (i, g, 0, 0)),
                          pl.BlockSpec(lam_rows.shape, lambda i, g, j: (0, 0)),
                          pl.BlockSpec(subln.shape, lambda i, g, j: (0, 0))]
    return _attention_call(functools.partial(_diff_kernel, lam_init=lam_init), "diff_attention", qbt,
                           (kb, vbt, lam_rows, subln), specs, n_ctx=n_ctx, heads=DIFF_HEADS, bounded=bounded)


def _gdn_prep_kernel(c_ref, cp_ref, cn_ref, g_ref, cw_ref, gp_ref, q_ref, k_ref, v_ref, bg_ref, *, ctx_tiles, nt):
    t = pl.program_id(1)
    x = c_ref[0]
    first = jnp.logical_or(t == 0, t == ctx_tiles)
    last = jnp.logical_or(t == ctx_tiles - 1, t == nt - 1)
    prev_row = jnp.where(first, 0.0, cp_ref[0, 7:8, :])
    next_row = jnp.where(last, 0.0, cn_ref[0, 0:1, :])
    ridx = lax.broadcasted_iota(jnp.int32, x.shape, 0)
    xm = jnp.where(ridx == 0, prev_row, pltpu.roll(x, 1, axis=0))
    xp = jnp.where(ridx == TM - 1, next_row, pltpu.roll(x, TM - 1, axis=0))
    y = _silu(xm * cw_ref[0:1, :] + x * cw_ref[1:2, :] + xp * cw_ref[2:3, :])
    for h in range(GDN_HEADS):
        sl = slice(h * GDN_DK, (h + 1) * GDN_DK)
        qh = y[:, sl]
        q_ref[0, :, sl] = (qh * lax.rsqrt(jnp.sum(qh * qh, axis=-1, keepdims=True) + EPS)
                           * (GDN_DK ** -0.5)).astype(BF16)
        kh = y[:, 512 + h * GDN_DK:512 + (h + 1) * GDN_DK]
        k_ref[0, :, sl] = (kh * lax.rsqrt(jnp.sum(kh * kh, axis=-1, keepdims=True) + EPS)).astype(BF16)
    v_ref[0] = y[:, 1024:1536].astype(BF16)
    gx = g_ref[0]
    lane = lax.broadcasted_iota(jnp.int32, gx.shape, 1)
    a = gx + gp_ref[1:2, :]
    softplus = jnp.maximum(a, 0.0) + jnp.log(1.0 + jnp.exp(-jnp.abs(a)))
    gdec = -jnp.exp(gp_ref[0:1, :]) * softplus
    bg_ref[0] = jnp.where(lane < 8, _sigmoid(gx), jnp.where(lane < 16, gdec, 0.0))


def _gdn_prep(cqkv, gates, conv_w8, gparams, *, ctx_tiles):
    b, t, cch = cqkv.shape
    nt = t // TM
    r8 = TM // 8
    tile = lambda width: pl.BlockSpec((1, TM, width), lambda i, j: (i, j, 0))
    return pl.pallas_call(
        functools.partial(_gdn_prep_kernel, ctx_tiles=ctx_tiles, nt=nt),
        grid=(b, nt),
        in_specs=[tile(cch),
                  pl.BlockSpec((1, 8, cch), lambda i, j: (i, jnp.maximum(j * r8 - 1, 0), 0)),
                  pl.BlockSpec((1, 8, cch), lambda i, j: (i, jnp.minimum((j + 1) * r8, nt * r8 - 1), 0)),
                  tile(LANES),
                  pl.BlockSpec(conv_w8.shape, lambda i, j: (0, 0)),
                  pl.BlockSpec(gparams.shape, lambda i, j: (0, 0))],
        out_specs=(tile(512), tile(512), tile(512), tile(LANES)),
        out_shape=(jax.ShapeDtypeStruct((b, t, 512), BF16), jax.ShapeDtypeStruct((b, t, 512), BF16),
                   jax.ShapeDtypeStruct((b, t, 512), BF16), jax.ShapeDtypeStruct((b, t, LANES), F32)),
        compiler_params=_cparams(("parallel", "parallel")),
        name="gdn_prep",
    )(cqkv, cqkv, cqkv, gates, conv_w8, gparams)


def _unit_tri_inverses(mats):
    n = mats[0].shape[0]
    r = lax.broadcasted_iota(jnp.int32, (n, n), 0)
    c = lax.broadcasted_iota(jnp.int32, (n, n), 1)
    eye = jnp.where(r == c, 1.0, 0.0)
    ps = [-a for a in mats]
    ts = [eye + p for p in ps]
    for _ in range(5):
        ps = [_mm3(p, p) for p in ps]
        ts = [t + _mm3(t, p) for t, p in zip(ts, ps)]
    return ts


def _gdn_chunk_kernel(q_ref, k_ref, v_ref, bg_ref, l1_ref, l2_ref, u_ref, gl_ref):
    L = GDN_CHUNK
    r = lax.broadcasted_iota(jnp.int32, (L, L), 0)
    c = lax.broadcasted_iota(jnp.int32, (L, L), 1)
    chains = [(n, d, h) for n in range(GDN_CHUNKS_PER_STEP) for d in range(2) for h in range(GDN_HEADS)]
    incl = [(r >= c), (r <= c)]
    strict = [(r > c), (r < c)]
    rows = [slice(n * L, (n + 1) * L) for n in range(GDN_CHUNKS_PER_STEP)]
    bgs = [bg_ref[0, rw, :] for rw in rows]
    gcs = [[_mm_exact_lhs(jnp.where(m, 1.0, 0.0).astype(BF16), bg) for m in incl] for bg in bgs]
    gcts = [[g.T for g in gc] for gc in gcs]
    sl = [slice(h * GDN_DK, (h + 1) * GDN_DK) for h in range(GDN_HEADS)]
    qs = [[q_ref[0, rw, s] for s in sl] for rw in rows]
    ks = [[k_ref[0, rw, s] for s in sl] for rw in rows]
    kfs = [[k.astype(F32) for k in kk] for kk in ks]
    gcc = [gcs[n][d][:, 8 + 4 * d + h:9 + 4 * d + h] for n, d, h in chains]
    gcr = [gcts[n][d][8 + 4 * d + h:9 + 4 * d + h, :] for n, d, h in chains]
    dec = [jnp.where(incl[d], jnp.exp(jnp.where(incl[d], gcc[i] - gcr[i], 0.0)), 0.0) for i, (n, d, h) in enumerate(chains)]
    beta = [bgs[n][:, 4 * d + h:4 * d + h + 1] for n, d, h in chains]
    kbeta = [kfs[n][h] * beta[i] for i, (n, d, h) in enumerate(chains)]
    amat = [jnp.where(strict[d], _mm_nt(kbeta[i], ks[n][h]) * dec[i], 0.0) for i, (n, d, h) in enumerate(chains)]
    tinv = _unit_tri_inverses(amat)
    eg = [jnp.exp(g) for g in gcc]
    rhs = [jnp.concatenate([v_ref[0, rows[n], sl[h]].astype(F32) * beta[i], kbeta[i] * eg[i]], axis=1)
           for i, (n, d, h) in enumerate(chains)]
    sol = [_mm3(t, x) for t, x in zip(tinv, rhs)]
    qk = [[_mm_nt(qs[n][h], ks[n][h]) for h in range(GDN_HEADS)] for n in range(GDN_CHUNKS_PER_STEP)]
    for i, (n, d, h) in enumerate(chains):
        last = L - 1 if d == 0 else 0
        glast = gcc[i][last:last + 1, :]
        kdec = kfs[n][h] * jnp.exp(glast - gcc[i])
        u_ref[0, d, n, h] = sol[i][:, :GDN_DK]
        l1_ref[0, d, n, h, 0:L, :] = sol[i][:, GDN_DK:].astype(BF16)
        l1_ref[0, d, n, h, L:2 * L, :] = (qs[n][h].astype(F32) * eg[i]).astype(BF16)
        l2_ref[0, d, n, h, 0:L, :] = (qk[n][h] * dec[i]).astype(BF16)
        l2_ref[0, d, n, h, L:, :] = kdec.T.astype(BF16)
        gl_ref[0, d, n, h:h + 1, :] = jnp.broadcast_to(jnp.exp(glast), (1, LANES))
    for d in range(2):
        for n in range(GDN_CHUNKS_PER_STEP):
            gl_ref[0, d, n, GDN_HEADS:, :] = jnp.zeros((8 - GDN_HEADS, LANES), F32)


GDN_CHUNKS_PER_STEP = 2


def _gdn_chunks(qg, kg, vg, bg):
    b, t, _ = qg.shape
    nc = t // GDN_CHUNK
    L = GDN_CHUNK
    cps = GDN_CHUNKS_PER_STEP
    assert nc % cps == 0
    tile = lambda width: pl.BlockSpec((1, cps * L, width), lambda i, j: (i, j, 0))
    return pl.pallas_call(
        _gdn_chunk_kernel,
        grid=(b, nc // cps),
        in_specs=[tile(512), tile(512), tile(512), tile(LANES)],
        out_specs=(pl.BlockSpec((1, 2, cps, 4, 2 * L, GDN_DK), lambda i, j: (i, 0, j, 0, 0, 0)),
                   pl.BlockSpec((1, 2, cps, 4, L + GDN_DK, L), lambda i, j: (i, 0, j, 0, 0, 0)),
                   pl.BlockSpec((1, 2, cps, 4, L, GDN_DK), lambda i, j: (i, 0, j, 0, 0, 0)),
                   pl.BlockSpec((1, 2, cps, 8, LANES), lambda i, j: (i, 0, j, 0, 0))),
        out_shape=(jax.ShapeDtypeStruct((b, 2, nc, 4, 2 * L, GDN_DK), BF16),
                   jax.ShapeDtypeStruct((b, 2, nc, 4, L + GDN_DK, L), BF16),
                   jax.ShapeDtypeStruct((b, 2, nc, 4, L, GDN_DK), F32),
                   jax.ShapeDtypeStruct((b, 2, nc, 8, LANES), F32)),
        compiler_params=_cparams(("parallel", "parallel")),
        name="gdn_chunks",
    )(qg, kg, vg, bg)


def _gdn_scan_kernel(l1f_ref, l2f_ref, uf_ref, glf_ref, l1b_ref, l2b_ref, ub_ref, glb_ref,
                     of_ref, ob_ref, s_ref):
    L = GDN_CHUNK

    @pl.when(pl.program_id(1) == 0)
    def _():
        s_ref[...] = jnp.zeros_like(s_ref)

    dirs = ((l1f_ref, l2f_ref, uf_ref, glf_ref, of_ref), (l1b_ref, l2b_ref, ub_ref, glb_ref, ob_ref))
    chains = [(d, h) for d in range(2) for h in range(GDN_HEADS)]
    ss = [s_ref[i] for i in range(len(chains))]
    r1 = [_mm(dirs[d][0][0, 0, 0, h], ss[i]) for i, (d, h) in enumerate(chains)]
    v_new = [dirs[d][2][0, 0, 0, h] - r1[i][:L] for i, (d, h) in enumerate(chains)]
    r2 = [_mm(dirs[d][1][0, 0, 0, h], v_new[i]) for i, (d, h) in enumerate(chains)]
    for i, (d, h) in enumerate(chains):
        dirs[d][4][0, :, h * GDN_DK:(h + 1) * GDN_DK] = r1[i][L:] + r2[i][:L]
        s_ref[i] = ss[i] * dirs[d][3][0, 0, 0, h:h + 1, :] + r2[i][L:]


def _gdn_scan(l1, l2, u, gl, *, ctx_chunks):
    b, _, nc = l1.shape[:3]
    L = GDN_CHUNK

    def rev(j):
        return jnp.where(j < ctx_chunks, ctx_chunks - 1 - j, nc - 1 - (j - ctx_chunks))

    def spec(a, d):
        blk = (1, 1, 1) + a.shape[3:]
        zeros = (0,) * (a.ndim - 3)
        if d == 0:
            return pl.BlockSpec(blk, lambda i, j: (i, 0, j) + zeros)
        return pl.BlockSpec(blk, lambda i, j: (i, 1, rev(j)) + zeros)

    return pl.pallas_call(
        _gdn_scan_kernel,
        grid=(b, nc),
        in_specs=[spec(l1, 0), spec(l2, 0), spec(u, 0), spec(gl, 0),
                  spec(l1, 1), spec(l2, 1), spec(u, 1), spec(gl, 1)],
        out_specs=(pl.BlockSpec((1, L, 512), lambda i, j: (i, j, 0)),
                   pl.BlockSpec((1, L, 512), lambda i, j: (i, rev(j), 0))),
        out_shape=(jax.ShapeDtypeStruct((b, nc * L, 512), F32), jax.ShapeDtypeStruct((b, nc * L, 512), F32)),
        scratch_shapes=[pltpu.VMEM((2 * GDN_HEADS, GDN_DK, GDN_DK), F32)],
        compiler_params=_cparams(("parallel", "arbitrary")),
        name="gdn_scan",
    )(l1, l2, u, gl, l1, l2, u, gl)


def _merge_kernel(x_ref, h_ref, yac_ref, yal_ref, ybc_ref, ybl_ref, of_ref, ob_ref, z_ref, mod_ref, gnw_ref, n2w_ref,
                  wg_ref, wb_ref, wo_ref, wr_ref, xo_ref, h2_ref, aff_ref, afft_ref, *, nb, ctx_tiles, d):
    m = _mod_row(mod_ref, nb, ctx_tiles)
    g1, sh2, sc2 = m[:, 2 * d:3 * d], m[:, 3 * d:4 * d], m[:, 4 * d:5 * d]
    o = of_ref[0] + ob_ref[0]
    z = z_ref[0].astype(F32)
    ygs = []
    for h in range(GDN_HEADS):
        sl = slice(h * GDN_DK, (h + 1) * GDN_DK)
        ygs.append(_rms(o[:, sl], gnw_ref[...]) * _silu(z[:, sl]))
    yg = jnp.concatenate(ygs, axis=1).astype(BF16)
    hb = h_ref[0]
    merged = jnp.zeros((TM, d), F32)
    is_ctx = pl.program_id(1) < ctx_tiles
    ya = jnp.where(is_ctx, yac_ref[0], yal_ref[0])
    yb = jnp.where(is_ctx, ybc_ref[0], ybl_ref[0])
    for i, y in enumerate((ya, yb, yg)):
        gate = _sigmoid(jnp.dot(hb, wg_ref[i], preferred_element_type=F32))
        merged = merged + gate * jnp.dot(y, wb_ref[i], preferred_element_type=F32)
    x = x_ref[0] + g1 * jnp.dot(merged.astype(BF16), wo_ref[...], preferred_element_type=F32)
    xo_ref[0] = x
    h2 = _rms(x, n2w_ref[...]) * (1.0 + sc2) + sh2
    h2b = h2.astype(BF16)
    h2_ref[0] = h2b
    logits = _mm3(h2, wr_ref[...])
    lane = lax.broadcasted_iota(jnp.int32, logits.shape, 1)
    logits = jnp.where(lane < N_EXPERTS, logits, -jnp.inf)
    e = jnp.exp(logits - jnp.max(logits, axis=-1, keepdims=True))
    aff = e / jnp.sum(e, axis=-1, keepdims=True)
    aff_ref[0] = aff
    afft_ref[0] = aff.T[:N_EXPERTS]


def _merge(xs, h, ya, yb, o_f, o_b, z, mod, gnw, n2w, wg, wb, wo, wr, *, nb, ctx_tiles):
    b, t, d = xs.shape
    nt = t // TM
    tile = lambda width: pl.BlockSpec((1, TM, width), lambda i, j: (i, j, 0))
    whole = lambda a: pl.BlockSpec(a.shape, lambda i, j: (0,) * a.ndim)
    (yac, yal), (ybc, ybl) = ya, yb
    ctx_tile = pl.BlockSpec((1, TM, 512), lambda i, j: (i, jnp.minimum(j, ctx_tiles - 1), 0))
    lat_tile = pl.BlockSpec((1, TM, 512), lambda i, j: (i, jnp.maximum(j - ctx_tiles, 0), 0))
    return pl.pallas_call(
        functools.partial(_merge_kernel, nb=nb, ctx_tiles=ctx_tiles, d=d),
        grid=(b, nt),
        in_specs=[tile(d), tile(d), ctx_tile, lat_tile, ctx_tile, lat_tile, tile(512), tile(512), tile(512),
                  whole(mod), whole(gnw), whole(n2w), whole(wg), whole(wb), whole(wo), whole(wr)],
        out_specs=(tile(d), tile(d), tile(LANES), pl.BlockSpec((1, N_EXPERTS, TM), lambda i, j: (i, 0, j))),
        out_shape=(jax.ShapeDtypeStruct((b, t, d), F32), jax.ShapeDtypeStruct((b, t, d), BF16),
                   jax.ShapeDtypeStruct((b, t, LANES), F32), jax.ShapeDtypeStruct((b, N_EXPERTS, t), F32)),
        compiler_params=_cparams(("parallel", "parallel")),
        name="merge",
    )(xs, h, yac, yal, ybc, ybl, o_f, o_b, z, mod, gnw, n2w, wg, wb, wo, wr)


ROW_ALIGN = 16
SLOT_GROUP = 64
FFN_CHUNKS = 3


def _moe_rows(caps, nt):
    chunk = -(-(sum(caps) + (ROW_ALIGN - 1) * nt) // (ROW_ALIGN * FFN_CHUNKS)) * ROW_ALIGN
    return chunk, FFN_CHUNKS * chunk + TM


def _route_kernel(afft_ref, post_ref, posc_ref, starts_ref, *, ctx_tiles, nt, cap_ctx, cap_lat):
    a = afft_ref[0]
    bits = pltpu.bitcast(a, jnp.int32)
    lane = lax.broadcasted_iota(jnp.int32, a.shape, 1)
    isctx = lane < ctx_tiles * TM

    def bisect(i, carry):
        tc, tl = carry
        bit = jnp.left_shift(jnp.int32(1), 30 - i)
        cc, cl = tc | bit, tl | bit
        ge = bits >= jnp.where(isctx, cc, cl)
        n_c = jnp.sum(jnp.where(jnp.logical_and(ge, isctx), 1.0, 0.0), axis=-1, keepdims=True)
        n_l = jnp.sum(jnp.where(jnp.logical_and(ge, jnp.logical_not(isctx)), 1.0, 0.0), axis=-1, keepdims=True)
        return jnp.where(n_c >= cap_ctx, cc, tc), jnp.where(n_l >= cap_lat, cl, tl)

    zero = jnp.zeros((N_EXPERTS, 1), jnp.int32)
    tc, tl = lax.fori_loop(0, 31, bisect, (zero, zero))
    thr = jnp.where(isctx, tc, tl)
    gt = bits > thr
    eq = bits == thr
    n_gt_c = jnp.sum(jnp.where(jnp.logical_and(gt, isctx), 1.0, 0.0), axis=-1, keepdims=True)
    n_gt_l = jnp.sum(jnp.where(jnp.logical_and(gt, jnp.logical_not(isctx)), 1.0, 0.0), axis=-1, keepdims=True)
    r = lax.broadcasted_iota(jnp.int32, (TM, TM), 0)
    c = lax.broadcasted_iota(jnp.int32, (TM, TM), 1)
    upper = jnp.where(r < c, 1.0, 0.0).astype(BF16)
    lane_t = lax.broadcasted_iota(jnp.int32, (N_EXPERTS, LANES), 1)
    starts = jnp.zeros((N_EXPERTS, LANES), F32)
    start = jnp.zeros((N_EXPERTS, 1), F32)
    eq_seen = jnp.zeros((N_EXPERTS, 1), F32)
    for j in range(nt):
        sl = slice(j * TM, (j + 1) * TM)
        if j == ctx_tiles:
            eq_seen = jnp.zeros((N_EXPERTS, 1), F32)
        need = (cap_ctx - n_gt_c) if j < ctx_tiles else (cap_lat - n_gt_l)
        eq_j = jnp.where(eq[:, sl], 1.0, 0.0)
        rank = eq_seen + _mm(eq_j, upper)
        eq_seen = eq_seen + jnp.sum(eq_j, axis=-1, keepdims=True)
        sel = jnp.logical_or(gt[:, sl], jnp.logical_and(eq[:, sl], rank < need))
        sel_f = jnp.where(sel, 1.0, 0.0)
        pos = jnp.where(sel, start + _mm(sel_f, upper), -1.0)
        post_ref[0, :, sl] = pos
        posc_ref[0, sl, :] = jnp.concatenate([pos, jnp.full((LANES - N_EXPERTS, TM), -1.0, F32)], axis=0).T
        starts = jnp.where(lane_t == j, start, starts)
        cnt = jnp.sum(sel_f, axis=-1, keepdims=True)
        start = start + jnp.floor((cnt + (ROW_ALIGN - 1)) * (1.0 / ROW_ALIGN)) * ROW_ALIGN
    starts_ref[0] = jnp.where(lane_t == nt, start, starts).astype(jnp.int32)


def _route(afft, *, ctx_tiles, cap_ctx, cap_lat):
    b, e, t = afft.shape
    nt = t // TM
    return pl.pallas_call(
        functools.partial(_route_kernel, ctx_tiles=ctx_tiles, nt=nt, cap_ctx=cap_ctx, cap_lat=cap_lat),
        grid=(b,),
        in_specs=[pl.BlockSpec((1, e, t), lambda i: (i, 0, 0))],
        out_specs=(pl.BlockSpec((1, e, t), lambda i: (i, 0, 0)),
                   pl.BlockSpec((1, t, LANES), lambda i: (i, 0, 0)),
                   pl.BlockSpec((1, e, LANES), lambda i: (i, 0, 0))),
        out_shape=(jax.ShapeDtypeStruct((b, e, t), F32), jax.ShapeDtypeStruct((b, t, LANES), F32),
                   jax.ShapeDtypeStruct((b, e, LANES), jnp.int32)),
        compiler_params=_cparams(("parallel",)),
        name="moe_route",
    )(afft)


def _moe_ffn_kernel(starts_ref, h2_ref, post_ref, wg_ref, wu_ref, wd_ref, y_ref, xin_ref, *, nt, unroll, chunk, n_picks):
    b, e = pl.program_id(0), pl.program_id(1)
    xin_ref[n_picks:, :] = jnp.zeros((xin_ref.shape[0] - n_picks, xin_ref.shape[1]), F32)
    slot = lax.broadcasted_iota(jnp.int32, (SLOT_GROUP, TM), 0).astype(F32)

    def picks(j, g):
        base = (b * N_EXPERTS + e) * LANES + j
        off = starts_ref[base]
        tok = pl.multiple_of(j * TM, TM)
        pos = post_ref[0, pl.ds(e, 1), pl.ds(tok, TM)] - (off + g * SLOT_GROUP).astype(F32)
        onehot = jnp.where(pos == slot, 1.0, 0.0).astype(BF16)
        blk = jnp.dot(onehot, h2_ref[0, pl.ds(tok, TM), :], preferred_element_type=F32)
        return pl.multiple_of(off + g * SLOT_GROUP, ROW_ALIGN), starts_ref[base + 1] - off - g * SLOT_GROUP, blk

    def first_groups(i, carry):
        blocks = [picks(i * unroll + k, 0) for k in range(unroll)]
        for row0, _, blk in blocks:
            xin_ref[pl.ds(row0, SLOT_GROUP), :] = blk
        return carry

    lax.fori_loop(0, nt // unroll, first_groups, 0)

    def more_groups(j, carry):
        def group(g, carry):
            row0, n_real, blk = picks(j, g)
            keep = lax.broadcasted_iota(jnp.int32, blk.shape, 0) < n_real
            xin_ref[pl.ds(row0, SLOT_GROUP), :] = jnp.where(keep, blk, xin_ref[pl.ds(row0, SLOT_GROUP), :])
            return carry

        base = (b * N_EXPERTS + e) * LANES + j
        n_groups = (starts_ref[base + 1] - starts_ref[base] + (SLOT_GROUP - 1)) // SLOT_GROUP
        return lax.fori_loop(1, n_groups, group, carry)

    lax.fori_loop(0, nt, more_groups, 0)

    def ffn(c, carry):
        rows = pl.ds(pl.multiple_of(c * chunk, ROW_ALIGN), chunk)
        x = xin_ref[rows, :].astype(BF16)
        hid = _silu(jnp.dot(x, wg_ref[0], preferred_element_type=F32)) * jnp.dot(x, wu_ref[0], preferred_element_type=F32)
        y_ref[0, 0, rows, :] = jnp.dot(hid.astype(BF16), wd_ref[0], preferred_element_type=F32).astype(BF16)
        return carry

    lax.fori_loop(0, FFN_CHUNKS, ffn, 0)
    y_ref[0, 0, FFN_CHUNKS * chunk:, :] = jnp.zeros((TM, y_ref.shape[3]), BF16)


def _moe_ffn(starts, h2, post, wg, wu, wd, *, chunk, rows, n_picks):
    b, t, d = h2.shape
    nt = t // TM
    e, _, ff = wg.shape
    grid_spec = pltpu.PrefetchScalarGridSpec(
        num_scalar_prefetch=1,
        grid=(b, e),
        in_specs=[pl.BlockSpec((1, t, d), lambda i, k, s: (i, 0, 0), pipeline_mode=pl.Buffered(1)),
                  pl.BlockSpec((1, e, t), lambda i, k, s: (i, 0, 0)),
                  pl.BlockSpec((1, d, ff), lambda i, k, s: (k, 0, 0)),
                  pl.BlockSpec((1, d, ff), lambda i, k, s: (k, 0, 0)),
                  pl.BlockSpec((1, ff, d), lambda i, k, s: (k, 0, 0))],
        out_specs=pl.BlockSpec((1, 1, rows, d), lambda i, k, s: (i, k, 0, 0)),
        scratch_shapes=[pltpu.VMEM((rows, d), F32)],
    )
    return pl.pallas_call(
        functools.partial(_moe_ffn_kernel, nt=nt, unroll=max(u for u in range(1, 12) if nt % u == 0), chunk=chunk,
                          n_picks=n_picks),
        grid_spec=grid_spec,
        out_shape=jax.ShapeDtypeStruct((b, e, rows, d), BF16),
        compiler_params=pltpu.CompilerParams(dimension_semantics=("parallel", "arbitrary"),
                                             vmem_limit_bytes=MOE_FFN_VMEM_LIMIT),
        name="moe_ffn",
    )(starts, h2, post, wg, wu, wd)


def _moe_combine_kernel(starts_ref, y_ref, posc_ref, aff_ref, x_ref, g2_ref, o_ref, *, nb, ctx_tiles):
    b, j = pl.program_id(0), pl.program_id(2)
    slot = lax.broadcasted_iota(jnp.int32, (TM, TM), 1).astype(F32)
    posc, aff = posc_ref[0], aff_ref[0]
    acc = jnp.zeros(o_ref.shape[1:], F32)
    for e in range(N_EXPERTS):
        off = pl.multiple_of(starts_ref[(b * N_EXPERTS + e) * LANES + j], ROW_ALIGN)
        pos = posc[:, e:e + 1]
        onehot = jnp.where(pos - off.astype(F32) == slot, 1.0, 0.0).astype(BF16)
        w = jnp.where(pos >= 0.0, aff[:, e:e + 1], 0.0)
        acc = acc + w * jnp.dot(onehot, y_ref[0, e, pl.ds(off, TM), :], preferred_element_type=F32)
    row = jnp.where(j < ctx_tiles, nb, b)
    o_ref[0] = x_ref[0] + g2_ref[pl.ds(row, 1), :] * acc


def _moe_combine(starts, y, posc, aff, xs, g2, *, nb, ctx_tiles, dsplit=2):
    b, t, d = xs.shape
    e, rows = y.shape[1:3]
    dq = d // dsplit
    grid_spec = pltpu.PrefetchScalarGridSpec(
        num_scalar_prefetch=1,
        grid=(b, dsplit, t // TM),
        in_specs=[pl.BlockSpec((1, e, rows, dq), lambda i, q, j, s: (i, 0, 0, q), pipeline_mode=pl.Buffered(1)),
                  pl.BlockSpec((1, TM, LANES), lambda i, q, j, s: (i, j, 0)),
                  pl.BlockSpec((1, TM, LANES), lambda i, q, j, s: (i, j, 0)),
                  pl.BlockSpec((1, TM, dq), lambda i, q, j, s: (i, j, q)),
                  pl.BlockSpec((MOD_ROWS, dq), lambda i, q, j, s: (0, q))],
        out_specs=pl.BlockSpec((1, TM, dq), lambda i, q, j, s: (i, j, q)),
    )
    return pl.pallas_call(
        functools.partial(_moe_combine_kernel, nb=nb, ctx_tiles=ctx_tiles),
        grid_spec=grid_spec,
        out_shape=jax.ShapeDtypeStruct((b, t, d), F32),
        compiler_params=_cparams(("parallel", "parallel", "parallel")),
        name="moe_combine",
    )(starts, y, posc, aff, xs, g2)


def _lambda_init(layer_idx):
    return 0.8 - 0.6 * math.exp(-0.3 * layer_idx)


def _rope_tables(n_lat, n_ctx):
    rows = n_lat // GRID_W
    row = jnp.repeat(jnp.arange(rows, dtype=F32), GRID_W)
    col = jnp.tile(jnp.arange(GRID_W, dtype=F32), rows)
    axis_dim = HEAD_DIM // 2
    inv_freq = ROPE_THETA ** (-jnp.arange(0, axis_dim, 2, dtype=F32) / axis_dim)
    ang_r, ang_c = row[:, None] * inv_freq, col[:, None] * inv_freq
    cos = jnp.concatenate([jnp.cos(ang_r)] * 2 + [jnp.cos(ang_c)] * 2, axis=-1)
    sin = jnp.concatenate([-jnp.sin(ang_r), jnp.sin(ang_r), -jnp.sin(ang_c), jnp.sin(ang_c)], axis=-1)
    cos = jnp.concatenate([jnp.ones((n_ctx, HEAD_DIM), F32), cos], axis=0)
    sin = jnp.concatenate([jnp.zeros((n_ctx, HEAD_DIM), F32), sin], axis=0)
    return jnp.tile(cos, (1, 2)), jnp.tile(sin, (1, 2))


def kernel(x, c, ctx, c_ctx, w_mod, b_mod, norm1_w, norm2_w, w_in, gqa_q_norm, gqa_k_norm, diff_q_norm,
           diff_k_norm, diff_lambda_q1, diff_lambda_k1, diff_lambda_q2, diff_lambda_k2, diff_subln,
           gdn_conv_w, gdn_a_log, gdn_dt_bias, gdn_norm_w, w_merge_gate, w_branch, w_out, w_router,
           w_exp_gate, w_exp_up, w_exp_down):
    nb, n_lat, d = x.shape
    n_ctx = ctx.shape[1]
    depth = w_mod.shape[0]
    assert n_ctx % TM == 0 and n_lat % TM == 0 and nb < MOD_ROWS and d == 1024
    ctx_tiles = n_ctx // TM
    ctx_chunks = n_ctx // GDN_CHUNK

    cos_t, sin_t = _rope_tables(n_lat, n_ctx)
    cc = jnp.zeros((MOD_ROWS, d), F32).at[:nb].set(c).at[nb].set(c_ctx)
    mods = _modulation(cc, w_mod, b_mod)
    half = lax.broadcasted_iota(jnp.int32, (LANES, LANES), 0) // HEAD_DIM
    bd = (half == half.T).astype(BF16)

    xs = jnp.concatenate([ctx, x], axis=1)
    scale = HEAD_DIM ** -0.5 * math.log2(math.e)
    for li in range(depth):
        lam_init = _lambda_init(li)
        w_in_p = jnp.pad(w_in[li], ((0, 0), (0, 4480 - w_in.shape[-1]))).astype(BF16)
        qkn = jnp.stack([jnp.tile(gqa_q_norm[li] * scale, 2), jnp.tile(gqa_k_norm[li], 2),
                         jnp.tile(diff_q_norm[li] * scale, 2), jnp.tile(diff_k_norm[li], 2)])
        (h, qa, ka, va, qb, kb, vb, cqkv, z, gates) = _inproj(
            xs, mods[li], norm1_w[li][None, :], w_in_p, cos_t, sin_t, qkn, bd, nb=nb, ctx_tiles=ctx_tiles)

        def score_bound(qn, kn):
            return HEAD_DIM * jnp.max(jnp.abs(qn * scale)) * jnp.max(jnp.abs(kn))

        ya = lax.cond(score_bound(gqa_q_norm[li], gqa_k_norm[li]) <= SCORE_LIMIT,
                      functools.partial(_gqa_attention, n_ctx=n_ctx, bounded=True),
                      functools.partial(_gqa_attention, n_ctx=n_ctx, bounded=False), qa, ka, va)
        lam_rows = jnp.stack([jnp.concatenate([diff_lambda_q1[li], diff_lambda_q2[li]]),
                              jnp.concatenate([diff_lambda_k1[li], diff_lambda_k2[li]])])
        yb = lax.cond(score_bound(diff_q_norm[li], diff_k_norm[li]) <= SCORE_LIMIT,
                      functools.partial(_diff_attention, n_ctx=n_ctx, lam_init=lam_init, bounded=True),
                      functools.partial(_diff_attention, n_ctx=n_ctx, lam_init=lam_init, bounded=False),
                      qb, kb, vb, lam_rows, diff_subln[li][None, :])

        conv_w8 = jnp.pad(gdn_conv_w[li], ((0, 5), (0, 0)))
        gparams = jnp.zeros((2, LANES), F32)
        gparams = gparams.at[0, 8:16].set(gdn_a_log[li].reshape(-1)).at[1, 8:16].set(gdn_dt_bias[li].reshape(-1))
        qg, kg, vg, bg = _gdn_prep(cqkv, gates, conv_w8, gparams, ctx_tiles=ctx_tiles)
        l1, l2, u, gl = _gdn_chunks(qg, kg, vg, bg)
        o_f, o_b = _gdn_scan(l1, l2, u, gl, ctx_chunks=ctx_chunks)

        wr = jnp.pad(w_router[li], ((0, 0), (0, LANES - N_EXPERTS)))
        xs, h2, aff, afft = _merge(xs, h, ya, yb, o_f, o_b, z, mods[li], gdn_norm_w[li][None, :],
                                   norm2_w[li][None, :], w_merge_gate[li].astype(BF16), w_branch[li].astype(BF16),
                                   w_out[li].astype(BF16), wr, nb=nb, ctx_tiles=ctx_tiles)

        caps = (CAPACITY_FACTOR * n_ctx // N_EXPERTS, CAPACITY_FACTOR * n_lat // N_EXPERTS)
        chunk, rows = _moe_rows(caps, (n_ctx + n_lat) // TM)
        post, posc, starts = _route(afft, ctx_tiles=ctx_tiles, cap_ctx=caps[0], cap_lat=caps[1])
        starts = starts.reshape(-1)
        y = _moe_ffn(starts, h2, post, w_exp_gate[li].astype(BF16), w_exp_up[li].astype(BF16),
                     w_exp_down[li].astype(BF16), chunk=chunk, rows=rows, n_picks=sum(caps))
        xs = _moe_combine(starts, y, posc, aff, xs, mods[li, :, 5 * d:], nb=nb, ctx_tiles=ctx_tiles)
    return xs[:, n_ctx:]
```

```python
import functools
import math

import jax
import jax.numpy as jnp
from jax import lax
from jax.experimental import pallas as pl
from jax.experimental.pallas import tpu as pltpu

F32 = jnp.float32
BF16 = jnp.bfloat16

LANES = 128
TM = 256
GRID_W = 64
HEAD_DIM = 64
GQA_HEADS = 8
GQA_KV_HEADS = 2
DIFF_HEADS = 4
GDN_HEADS = 4
GDN_DK = 128
GDN_CHUNK = 64
N_EXPERTS = 16
CAPACITY_FACTOR = 2
ROPE_THETA = 10000.0
EPS = 1e-6
ONES_ROWS = 16
MOD_ROWS = 16
VMEM_LIMIT = 56 * 1024 * 1024
MOE_FFN_VMEM_LIMIT = 60 * 1024 * 1024


def _cparams(sem):
    return pltpu.CompilerParams(dimension_semantics=sem, vmem_limit_bytes=VMEM_LIMIT)


def _mm(a, b):
    return jnp.dot(a.astype(BF16), b.astype(BF16), preferred_element_type=F32)


def _mm_nt(a, b):
    return lax.dot_general(a.astype(BF16), b.astype(BF16), (((1,), (1,)), ((), ())),
                           preferred_element_type=F32)


def _split2(a):
    hi = a.astype(BF16)
    lo = (a - hi.astype(F32)).astype(BF16)
    return hi, lo


def _mm3(a, b):
    ah, al = _split2(a)
    bh, bl = _split2(b)
    return _mm(ah, bh) + _mm(ah, bl) + _mm(al, bh)


def _mm_exact_lhs(a_bf16, b):
    b1 = b.astype(BF16)
    r1 = b - b1.astype(F32)
    b2 = r1.astype(BF16)
    b3 = (r1 - b2.astype(F32)).astype(BF16)
    return _mm(a_bf16, b1) + _mm(a_bf16, b2) + _mm(a_bf16, b3)


def _sigmoid(x):
    return 1.0 / (1.0 + jnp.exp(-x))


def _silu(x):
    return x * _sigmoid(x)


def _mod_kernel(a_ref, w_ref, b_ref, o_ref):
    a = _silu(a_ref[...])
    o_ref[0] = _mm3(a, w_ref[0]) + b_ref[0]


def _modulation(cc, w_mod, b_mod):
    depth, d, d6 = w_mod.shape
    nj = d6 // d
    return pl.pallas_call(
        _mod_kernel,
        grid=(depth, nj),
        in_specs=[pl.BlockSpec((MOD_ROWS, d), lambda l, j: (0, 0)),
                  pl.BlockSpec((1, d, d), lambda l, j: (l, 0, j)),
                  pl.BlockSpec((1, 1, d), lambda l, j: (l, 0, j))],
        out_specs=pl.BlockSpec((1, MOD_ROWS, d), lambda l, j: (l, 0, j)),
        out_shape=jax.ShapeDtypeStruct((depth, MOD_ROWS, d6), F32),
        compiler_params=_cparams(("parallel", "parallel")),
        name="modulation",
    )(cc, w_mod, b_mod.reshape(depth, 1, d6))


def _mod_row(mod_ref, nb, ctx_tiles):
    b = pl.program_id(0)
    t = pl.program_id(1)
    row = jnp.where(t < ctx_tiles, nb, b)
    return mod_ref[pl.ds(row, 1), :]


def _rms(x, w):
    return x * lax.rsqrt(jnp.mean(x * x, axis=-1, keepdims=True) + EPS) * w


def _half_norm_rope(t, w, bd, cos, sin):
    ssq = _mm(t * t, bd)
    tn = t * lax.rsqrt(ssq * (1.0 / HEAD_DIM) + EPS) * w
    lane = lax.broadcasted_iota(jnp.int32, tn.shape, 1)
    partner = jnp.where((lane % 32) < 16, pltpu.roll(tn, LANES - 16, axis=1), pltpu.roll(tn, 16, axis=1))
    return tn * cos + partner * sin


def _inproj_kernel(x_ref, mod_ref, nw_ref, w_ref, cos_ref, sin_ref, qkn_ref, bd_ref,
                   h_ref, qa_ref, ka_ref, va_ref, qb_ref, kb_ref, vb_ref, c_ref, z_ref, g_ref,
                   *, nb, ctx_tiles, d):
    m = _mod_row(mod_ref, nb, ctx_tiles)
    sh, sc = m[:, 0:d], m[:, d:2 * d]
    h = _rms(x_ref[0], nw_ref[...]) * (1.0 + sc) + sh
    hb = h.astype(BF16)
    h_ref[0] = hb
    cos, sin, bd = cos_ref[...], sin_ref[...], bd_ref[...]

    def proj(lo, width):
        return jnp.dot(hb, w_ref[:, lo:lo + width], preferred_element_type=F32)

    ones_rows = jnp.ones((ONES_ROWS, TM), BF16)
    group = lambda t, g: t[:, g * LANES:(g + 1) * LANES]
    q_all, kv = proj(0, 512), proj(512, 2 * LANES)
    for g in range(4):
        t = _half_norm_rope(group(q_all, g), qkn_ref[0:1, :], bd, cos, sin)
        qa_ref[0, g * LANES:(g + 1) * LANES, :] = t.T.astype(BF16)
    ka_ref[0] = _half_norm_rope(group(kv, 0), qkn_ref[1:2, :], bd, cos, sin).astype(BF16)
    vt = group(kv, 1).T.astype(BF16)
    for j in range(GQA_KV_HEADS):
        va_ref[0, j, 0:HEAD_DIM, :] = vt[j * HEAD_DIM:(j + 1) * HEAD_DIM]
        va_ref[0, j, HEAD_DIM:HEAD_DIM + ONES_ROWS, :] = ones_rows
    q_all, k_all, v_all = proj(768, 512), proj(1280, 512), proj(1792, 512)
    for g in range(4):
        t = _half_norm_rope(group(q_all, g), qkn_ref[2:3, :], bd, cos, sin)
        qb_ref[0, g * LANES:(g + 1) * LANES, :] = t.T.astype(BF16)
        kb_ref[0, :, g * LANES:(g + 1) * LANES] = _half_norm_rope(group(k_all, g), qkn_ref[3:4, :], bd, cos,
                                                                  sin).astype(BF16)
        vb_ref[0, g, 0:LANES, :] = group(v_all, g).T.astype(BF16)
        vb_ref[0, g, LANES:LANES + ONES_ROWS, :] = ones_rows
    for g in range(3):
        c_ref[0, :, g * 512:(g + 1) * 512] = proj(2304 + g * 512, 512)
    z_ref[0] = proj(3840, 512).astype(BF16)
    g_ref[0] = proj(4352, LANES)


def _inproj(xs, mod, nw, w_in_p, cos_t, sin_t, qkn, bd, *, nb, ctx_tiles):
    b, t, d = xs.shape
    nt = t // TM
    tile = lambda width: pl.BlockSpec((1, TM, width), lambda i, j: (i, j, 0))
    whole = lambda a: pl.BlockSpec(a.shape, lambda i, j: (0,) * a.ndim)
    out_shape = (
        jax.ShapeDtypeStruct((b, t, d), BF16),
        jax.ShapeDtypeStruct((b, 512, t), BF16),
        jax.ShapeDtypeStruct((b, t, LANES), BF16),
        jax.ShapeDtypeStruct((b, 2, HEAD_DIM + ONES_ROWS, t), BF16),
        jax.ShapeDtypeStruct((b, 512, t), BF16),
        jax.ShapeDtypeStruct((b, t, 512), BF16),
        jax.ShapeDtypeStruct((b, 4, LANES + ONES_ROWS, t), BF16),
        jax.ShapeDtypeStruct((b, t, 1536), F32),
        jax.ShapeDtypeStruct((b, t, 512), BF16),
        jax.ShapeDtypeStruct((b, t, LANES), F32),
    )
    out_specs = (
        tile(d),
        pl.BlockSpec((1, 512, TM), lambda i, j: (i, 0, j)),
        tile(LANES),
        pl.BlockSpec((1, 2, HEAD_DIM + ONES_ROWS, TM), lambda i, j: (i, 0, 0, j)),
        pl.BlockSpec((1, 512, TM), lambda i, j: (i, 0, j)),
        tile(512),
        pl.BlockSpec((1, 4, LANES + ONES_ROWS, TM), lambda i, j: (i, 0, 0, j)),
        tile(1536), tile(512), tile(LANES),
    )
    return pl.pallas_call(
        functools.partial(_inproj_kernel, nb=nb, ctx_tiles=ctx_tiles, d=d),
        grid=(b, nt),
        in_specs=[tile(d), whole(mod), whole(nw), whole(w_in_p),
                  pl.BlockSpec((TM, LANES), lambda i, j: (j, 0)),
                  pl.BlockSpec((TM, LANES), lambda i, j: (j, 0)),
                  whole(qkn), whole(bd)],
        out_specs=out_specs,
        out_shape=out_shape,
        compiler_params=_cparams(("parallel", "parallel")),
        name="inproj",
    )(xs, mod, nw, w_in_p, cos_t, sin_t, qkn, bd)


def _softmax_step(st, vt, mx, acc):
    mnew = jnp.maximum(mx, jnp.max(st, axis=0, keepdims=True))
    p = jnp.exp2(st - mnew).astype(BF16)
    acc = acc * jnp.exp2(mx - mnew) + jnp.dot(vt, p, preferred_element_type=F32)
    return mnew, acc


def _attend_bounded(w, k_at, vt_at, n_ctx, n_lat, vrows):
    def step(off, n, acc):
        p = jnp.exp2(jnp.dot(k_at(off, n), w, preferred_element_type=F32)).astype(BF16)
        return acc + jnp.dot(vt_at(off, n), p, preferred_element_type=F32)

    acc = step(0, n_ctx, jnp.zeros((vrows, w.shape[1]), F32))
    tk = min(BOUNDED_TK, n_lat) if n_lat else 0
    for c in range(n_lat // tk if tk else 0):
        acc = step(n_ctx + c * tk, tk, acc)
    return acc


def _attend(w, k_at, vt_at, n_ctx, n_groups, tk, s_refs, vrows):
    unroll = len(s_refs)

    def scores(c):
        return jnp.dot(k_at(lat(c), tk), w, preferred_element_type=F32)

    def lat(c):
        return pl.multiple_of(n_ctx + c * tk, TM)

    mx = jnp.full((1, w.shape[1]), -jnp.inf, F32)
    acc = jnp.zeros((vrows, w.shape[1]), F32)
    mx, acc = _softmax_step(jnp.dot(k_at(0, n_ctx), w, preferred_element_type=F32), vt_at(0, n_ctx), mx, acc)
    if n_groups == 0:
        return acc

    def group(i, carry, prefetch):
        mx, acc = carry
        for u in range(unroll):
            c = i * unroll + u
            if u + 1 < unroll or prefetch:
                s_refs[(u + 1) % unroll][...] = scores(c + 1)
            mx, acc = _softmax_step(s_refs[u][...], vt_at(lat(c), tk), mx, acc)
        return mx, acc

    s_refs[0][...] = scores(0)
    carry = lax.fori_loop(0, n_groups - 1, functools.partial(group, prefetch=True), (mx, acc))
    return group(n_groups - 1, carry, False)[1]


SCORE_LIMIT = 64.0
BOUNDED_TK = 1024
ATTN_Q_BLOCKS = 2
ATTN_UNROLL = 4
ATTN_TK = 512


def _attn_chunk(n_lat):
    tk = min(ATTN_TK, max(TM, n_lat // ATTN_UNROLL))
    unroll = min(ATTN_UNROLL, n_lat // tk)
    assert n_lat % (unroll * tk) == 0 and tk % TM == 0 and n_lat % (ATTN_Q_BLOCKS * TM) == 0
    return tk, unroll, n_lat // (unroll * tk)


def _attention_call(kern, name, q_t, others, other_specs, *, n_ctx, heads, bounded):
    b, _, t = q_t.shape
    n_lat = t - n_ctx
    tk, unroll, n_groups = _attn_chunk(n_lat)
    nq, ctx_tiles = ATTN_Q_BLOCKS, n_ctx // TM
    tq = nq * TM
    q_specs = [pl.BlockSpec((1, LANES, TM), functools.partial(lambda i, g, j, u: (i, g, ctx_tiles + nq * j + u), u=u))
               for u in range(nq)]
    lat = pl.pallas_call(
        functools.partial(kern, nq=nq, n_ctx=n_ctx, n_lat=n_lat, n_groups=n_groups, tk=tk, bounded=bounded),
        grid=(b, heads, n_lat // tq),
        in_specs=q_specs + other_specs(t),
        out_specs=pl.BlockSpec((1, tq, LANES), lambda i, g, j: (i, j, g)),
        out_shape=jax.ShapeDtypeStruct((b, n_lat, 512), BF16),
        scratch_shapes=[] if bounded else [pltpu.VMEM((tk, 2 * tq), F32)] * unroll,
        compiler_params=_cparams(("parallel", "parallel", "parallel")),
        name=name,
    )(*([q_t] * nq), *others)
    ctx = pl.pallas_call(
        functools.partial(kern, nq=1, n_ctx=n_ctx, n_lat=0, n_groups=0, tk=tk, bounded=bounded),
        grid=(b, heads, ctx_tiles),
        in_specs=[pl.BlockSpec((1, LANES, TM), lambda i, g, j: (i, g, j))] + other_specs(n_ctx),
        out_specs=pl.BlockSpec((1, TM, LANES), lambda i, g, j: (i, j, g)),
        out_shape=jax.ShapeDtypeStruct((b, n_ctx, 512), BF16),
        compiler_params=_cparams(("parallel", "parallel", "parallel")),
        name=name + "_ctx",
    )(q_t, *others)
    return ctx, lat


def _gqa_kernel(*refs, nq, n_ctx, n_lat, n_groups, tk, bounded):
    qt_refs, (k_ref, vt_ref, o_ref), s_refs = refs[:nq], refs[nq:nq + 3], refs[nq + 3:]
    tq = nq * TM
    top = jnp.concatenate([r[0, :HEAD_DIM] for r in qt_refs] + [r[0, HEAD_DIM:] for r in qt_refs], axis=1)
    zero = jnp.zeros_like(top)
    w = jnp.where(pl.program_id(1) < 2, jnp.concatenate([top, zero], axis=0), jnp.concatenate([zero, top], axis=0))
    vrows = HEAD_DIM + ONES_ROWS
    k_at, vt_at = (lambda off, n: k_ref[0, pl.ds(off, n), :]), (lambda off, n: vt_ref[0, 0, :, pl.ds(off, n)])
    if bounded:
        acc = _attend_bounded(w, k_at, vt_at, n_ctx, n_lat, vrows)
    else:
        acc = _attend(w, k_at, vt_at, n_ctx, n_groups, tk, s_refs, vrows)
    o = acc[:HEAD_DIM] / acc[HEAD_DIM:HEAD_DIM + 1]
    o_ref[0] = jnp.concatenate([o[:, :tq], o[:, tq:]], axis=0).T.astype(BF16)


def _gqa_attention(qat, ka, vat, *, n_ctx, bounded):
    vrows = vat.shape[2]
    specs = lambda keys: [pl.BlockSpec((1, keys, LANES), lambda i, g, j: (i, 0, 0)),
                          pl.BlockSpec((1, 1, vrows, keys), lambda i, g, j: (i, g // 2, 0, 0))]
    return _attention_call(_gqa_kernel, "gqa_attention", qat, (ka, vat), specs, n_ctx=n_ctx, heads=4, bounded=bounded)


def _diff_kernel(*refs, nq, n_ctx, n_lat, n_groups, tk, bounded, lam_init):
    qt_refs, (k_ref, vt_ref, lam_ref, sw_ref, o_ref), s_refs = refs[:nq], refs[nq:nq + 5], refs[nq + 5:]
    tq = nq * TM
    qt = jnp.concatenate([r[0] for r in qt_refs], axis=1)
    zero = jnp.zeros((HEAD_DIM, tq), BF16)
    w = jnp.concatenate([jnp.concatenate([qt[:HEAD_DIM], zero], axis=0),
                         jnp.concatenate([zero, qt[HEAD_DIM:]], axis=0)], axis=1)
    vrows = LANES + ONES_ROWS
    k_at, vt_at = (lambda off, n: k_ref[0, pl.ds(off, n), :]), (lambda off, n: vt_ref[0, 0, :, pl.ds(off, n)])
    if bounded:
        acc = _attend_bounded(w, k_at, vt_at, n_ctx, n_lat, vrows)
    else:
        acc = _attend(w, k_at, vt_at, n_ctx, n_groups, tk, s_refs, vrows)
    prod = lam_ref[0:1, :] * lam_ref[1:2, :]
    low = lax.broadcasted_iota(jnp.int32, prod.shape, 1) < HEAD_DIM
    l1 = jnp.sum(jnp.where(low, prod, 0.0), axis=-1, keepdims=True)
    l2 = jnp.sum(jnp.where(low, 0.0, prod), axis=-1, keepdims=True)
    lam = jnp.exp(l1) - jnp.exp(l2) + lam_init
    o1 = acc[:LANES, :tq] / acc[LANES:LANES + 1, :tq]
    o2 = acc[:LANES, tq:] / acc[LANES:LANES + 1, tq:]
    o = (o1 - lam * o2).T
    o_ref[0] = (_rms(o, sw_ref[...]) * (1.0 - lam_init)).astype(BF16)


def _diff_attention(qbt, kb, vbt, lam_rows, subln, *, n_ctx, lam_init, bounded):
    vrows = vbt.shape[2]
    specs = lambda keys: [pl.BlockSpec((1, keys, LANES), lambda i, g, j: (i, 0, g)),
                          pl.BlockSpec((1, 1, vrows, keys), lambda i, g, j: (i, g, 0, 0)),
                          pl.BlockSpec(lam_rows.shape, lambda i, g, j: (0, 0)),
                          pl.BlockSpec(subln.shape, lambda i, g, j: (0, 0))]
    return _attention_call(functools.partial(_diff_kernel, lam_init=lam_init), "diff_attention", qbt,
                           (kb, vbt, lam_rows, subln), specs, n_ctx=n_ctx, heads=DIFF_HEADS, bounded=bounded)


def _gdn_prep_kernel(c_ref, cp_ref, cn_ref, g_ref, cw_ref, gp_ref, q_ref, k_ref, v_ref, bg_ref, *, ctx_tiles, nt):
    t = pl.program_id(1)
    x = c_ref[0]
    first = jnp.logical_or(t == 0, t == ctx_tiles)
    last = jnp.logical_or(t == ctx_tiles - 1, t == nt - 1)
    prev_row = jnp.where(first, 0.0, cp_ref[0, 7:8, :])
    next_row = jnp.where(last, 0.0, cn_ref[0, 0:1, :])
    ridx = lax.broadcasted_iota(jnp.int32, x.shape, 0)
    xm = jnp.where(ridx == 0, prev_row, pltpu.roll(x, 1, axis=0))
    xp = jnp.where(ridx == TM - 1, next_row, pltpu.roll(x, TM - 1, axis=0))
    y = _silu(xm * cw_ref[0:1, :] + x * cw_ref[1:2, :] + xp * cw_ref[2:3, :])
    for h in range(GDN_HEADS):
        sl = slice(h * GDN_DK, (h + 1) * GDN_DK)
        qh = y[:, sl]
        q_ref[0, :, sl] = (qh * lax.rsqrt(jnp.sum(qh * qh, axis=-1, keepdims=True) + EPS)
                           * (GDN_DK ** -0.5)).astype(BF16)
        kh = y[:, 512 + h * GDN_DK:512 + (h + 1) * GDN_DK]
        k_ref[0, :, sl] = (kh * lax.rsqrt(jnp.sum(kh * kh, axis=-1, keepdims=True) + EPS)).astype(BF16)
    v_ref[0] = y[:, 1024:1536].astype(BF16)
    gx = g_ref[0]
    lane = lax.broadcasted_iota(jnp.int32, gx.shape, 1)
    a = gx + gp_ref[1:2, :]
    softplus = jnp.maximum(a, 0.0) + jnp.log(1.0 + jnp.exp(-jnp.abs(a)))
    gdec = -jnp.exp(gp_ref[0:1, :]) * softplus
    bg_ref[0] = jnp.where(lane < 8, _sigmoid(gx), jnp.where(lane < 16, gdec, 0.0))


def _gdn_prep(cqkv, gates, conv_w8, gparams, *, ctx_tiles):
    b, t, cch = cqkv.shape
    nt = t // TM
    r8 = TM // 8
    tile = lambda width: pl.BlockSpec((1, TM, width), lambda i, j: (i, j, 0))
    return pl.pallas_call(
        functools.partial(_gdn_prep_kernel, ctx_tiles=ctx_tiles, nt=nt),
        grid=(b, nt),
        in_specs=[tile(cch),
                  pl.BlockSpec((1, 8, cch), lambda i, j: (i, jnp.maximum(j * r8 - 1, 0), 0)),
                  pl.BlockSpec((1, 8, cch), lambda i, j: (i, jnp.minimum((j + 1) * r8, nt * r8 - 1), 0)),
                  tile(LANES),
                  pl.BlockSpec(conv_w8.shape, lambda i, j: (0, 0)),
                  pl.BlockSpec(gparams.shape, lambda i, j: (0, 0))],
        out_specs=(tile(512), tile(512), tile(512), tile(LANES)),
        out_shape=(jax.ShapeDtypeStruct((b, t, 512), BF16), jax.ShapeDtypeStruct((b, t, 512), BF16),
                   jax.ShapeDtypeStruct((b, t, 512), BF16), jax.ShapeDtypeStruct((b, t, LANES), F32)),
        compiler_params=_cparams(("parallel", "parallel")),
        name="gdn_prep",
    )(cqkv, cqkv, cqkv, gates, conv_w8, gparams)


def _unit_tri_inverses(mats):
    n = mats[0].shape[0]
    r = lax.broadcasted_iota(jnp.int32, (n, n), 0)
    c = lax.broadcasted_iota(jnp.int32, (n, n), 1)
    eye = jnp.where(r == c, 1.0, 0.0)
    ps = [-a for a in mats]
    ts = [eye + p for p in ps]
    for _ in range(5):
        ps = [_mm3(p, p) for p in ps]
        ts = [t + _mm3(t, p) for t, p in zip(ts, ps)]
    return ts


def _gdn_chunk_kernel(q_ref, k_ref, v_ref, bg_ref, l1_ref, l2_ref, u_ref, gl_ref):
    L = GDN_CHUNK
    r = lax.broadcasted_iota(jnp.int32, (L, L), 0)
    c = lax.broadcasted_iota(jnp.int32, (L, L), 1)
    chains = [(n, d, h) for n in range(GDN_CHUNKS_PER_STEP) for d in range(2) for h in range(GDN_HEADS)]
    incl = [(r >= c), (r <= c)]
    strict = [(r > c), (r < c)]
    rows = [slice(n * L, (n + 1) * L) for n in range(GDN_CHUNKS_PER_STEP)]
    bgs = [bg_ref[0, rw, :] for rw in rows]
    gcs = [[_mm_exact_lhs(jnp.where(m, 1.0, 0.0).astype(BF16), bg) for m in incl] for bg in bgs]
    gcts = [[g.T for g in gc] for gc in gcs]
    sl = [slice(h * GDN_DK, (h + 1) * GDN_DK) for h in range(GDN_HEADS)]
    qs = [[q_ref[0, rw, s] for s in sl] for rw in rows]
    ks = [[k_ref[0, rw, s] for s in sl] for rw in rows]
    kfs = [[k.astype(F32) for k in kk] for kk in ks]
    gcc = [gcs[n][d][:, 8 + 4 * d + h:9 + 4 * d + h] for n, d, h in chains]
    gcr = [gcts[n][d][8 + 4 * d + h:9 + 4 * d + h, :] for n, d, h in chains]
    dec = [jnp.where(incl[d], jnp.exp(jnp.where(incl[d], gcc[i] - gcr[i], 0.0)), 0.0) for i, (n, d, h) in enumerate(chains)]
    beta = [bgs[n][:, 4 * d + h:4 * d + h + 1] for n, d, h in chains]
    kbeta = [kfs[n][h] * beta[i] for i, (n, d, h) in enumerate(chains)]
    amat = [jnp.where(strict[d], _mm_nt(kbeta[i], ks[n][h]) * dec[i], 0.0) for i, (n, d, h) in enumerate(chains)]
    tinv = _unit_tri_inverses(amat)
    eg = [jnp.exp(g) for g in gcc]
    rhs = [jnp.concatenate([v_ref[0, rows[n], sl[h]].astype(F32) * beta[i], kbeta[i] * eg[i]], axis=1)
           for i, (n, d, h) in enumerate(chains)]
    sol = [_mm3(t, x) for t, x in zip(tinv, rhs)]
    qk = [[_mm_nt(qs[n][h], ks[n][h]) for h in range(GDN_HEADS)] for n in range(GDN_CHUNKS_PER_STEP)]
    for i, (n, d, h) in enumerate(chains):
        last = L - 1 if d == 0 else 0
        glast = gcc[i][last:last + 1, :]
        kdec = kfs[n][h] * jnp.exp(glast - gcc[i])
        u_ref[0, d, n, h] = sol[i][:, :GDN_DK]
        l1_ref[0, d, n, h, 0:L, :] = sol[i][:, GDN_DK:].astype(BF16)
        l1_ref[0, d, n, h, L:2 * L, :] = (qs[n][h].astype(F32) * eg[i]).astype(BF16)
        l2_ref[0, d, n, h, 0:L, :] = (qk[n][h] * dec[i]).astype(BF16)
        l2_ref[0, d, n, h, L:, :] = kdec.T.astype(BF16)
        gl_ref[0, d, n, h:h + 1, :] = jnp.broadcast_to(jnp.exp(glast), (1, LANES))
    for d in range(2):
        for n in range(GDN_CHUNKS_PER_STEP):
            gl_ref[0, d, n, GDN_HEADS:, :] = jnp.zeros((8 - GDN_HEADS, LANES), F32)


GDN_CHUNKS_PER_STEP = 2


def _gdn_chunks(qg, kg, vg, bg):
    b, t, _ = qg.shape
    nc = t // GDN_CHUNK
    L = GDN_CHUNK
    cps = GDN_CHUNKS_PER_STEP
    assert nc % cps == 0
    tile = lambda width: pl.BlockSpec((1, cps * L, width), lambda i, j: (i, j, 0))
    return pl.pallas_call(
        _gdn_chunk_kernel,
        grid=(b, nc // cps),
        in_specs=[tile(512), tile(512), tile(512), tile(LANES)],
        out_specs=(pl.BlockSpec((1, 2, cps, 4, 2 * L, GDN_DK), lambda i, j: (i, 0, j, 0, 0, 0)),
                   pl.BlockSpec((1, 2, cps, 4, L + GDN_DK, L), lambda i, j: (i, 0, j, 0, 0, 0)),
                   pl.BlockSpec((1, 2, cps, 4, L, GDN_DK), lambda i, j: (i, 0, j, 0, 0, 0)),
                   pl.BlockSpec((1, 2, cps, 8, LANES), lambda i, j: (i, 0, j, 0, 0))),
        out_shape=(jax.ShapeDtypeStruct((b, 2, nc, 4, 2 * L, GDN_DK), BF16),
                   jax.ShapeDtypeStruct((b, 2, nc, 4, L + GDN_DK, L), BF16),
                   jax.ShapeDtypeStruct((b, 2, nc, 4, L, GDN_DK), F32),
                   jax.ShapeDtypeStruct((b, 2, nc, 8, LANES), F32)),
        compiler_params=_cparams(("parallel", "parallel")),
        name="gdn_chunks",
    )(qg, kg, vg, bg)


def _gdn_scan_kernel(l1f_ref, l2f_ref, uf_ref, glf_ref, l1b_ref, l2b_ref, ub_ref, glb_ref,
                     of_ref, ob_ref, s_ref):
    L = GDN_CHUNK

    @pl.when(pl.program_id(1) == 0)
    def _():
        s_ref[...] = jnp.zeros_like(s_ref)

    dirs = ((l1f_ref, l2f_ref, uf_ref, glf_ref, of_ref), (l1b_ref, l2b_ref, ub_ref, glb_ref, ob_ref))
    chains = [(d, h) for d in range(2) for h in range(GDN_HEADS)]
    ss = [s_ref[i] for i in range(len(chains))]
    for step in range(GDN_SCAN_CHUNKS):
        ns = (step, GDN_SCAN_CHUNKS - 1 - step)
        r1 = [_mm(dirs[d][0][0, 0, ns[d], h], ss[i]) for i, (d, h) in enumerate(chains)]
        v_new = [dirs[d][2][0, 0, ns[d], h] - r1[i][:L] for i, (d, h) in enumerate(chains)]
        r2 = [_mm(dirs[d][1][0, 0, ns[d], h], v_new[i]) for i, (d, h) in enumerate(chains)]
        for i, (d, h) in enumerate(chains):
            dirs[d][4][0, ns[d] * L:(ns[d] + 1) * L, h * GDN_DK:(h + 1) * GDN_DK] = r1[i][L:] + r2[i][:L]
        ss = [ss[i] * dirs[d][3][0, 0, ns[d], h:h + 1, :] + r2[i][L:] for i, (d, h) in enumerate(chains)]
    for i in range(len(chains)):
        s_ref[i] = ss[i]


GDN_SCAN_CHUNKS = 2


def _gdn_scan(l1, l2, u, gl, *, ctx_chunks):
    b, _, nc = l1.shape[:3]
    L = GDN_CHUNK
    cs = GDN_SCAN_CHUNKS
    assert nc % cs == 0 and ctx_chunks % cs == 0
    nblk, ctx_blk = nc // cs, ctx_chunks // cs

    def rev(j):
        return jnp.where(j < ctx_blk, ctx_blk - 1 - j, nblk - 1 - (j - ctx_blk))

    def spec(a, d):
        blk = (1, 1, cs) + a.shape[3:]
        zeros = (0,) * (a.ndim - 3)
        if d == 0:
            return pl.BlockSpec(blk, lambda i, j: (i, 0, j) + zeros)
        return pl.BlockSpec(blk, lambda i, j: (i, 1, rev(j)) + zeros)

    return pl.pallas_call(
        _gdn_scan_kernel,
        grid=(b, nblk),
        in_specs=[spec(l1, 0), spec(l2, 0), spec(u, 0), spec(gl, 0),
                  spec(l1, 1), spec(l2, 1), spec(u, 1), spec(gl, 1)],
        out_specs=(pl.BlockSpec((1, cs * L, 512), lambda i, j: (i, j, 0)),
                   pl.BlockSpec((1, cs * L, 512), lambda i, j: (i, rev(j), 0))),
        out_shape=(jax.ShapeDtypeStruct((b, nc * L, 512), F32), jax.ShapeDtypeStruct((b, nc * L, 512), F32)),
        scratch_shapes=[pltpu.VMEM((2 * GDN_HEADS, GDN_DK, GDN_DK), F32)],
        compiler_params=_cparams(("parallel", "arbitrary")),
        name="gdn_scan",
    )(l1, l2, u, gl, l1, l2, u, gl)


def _merge_kernel(x_ref, h_ref, yac_ref, yal_ref, ybc_ref, ybl_ref, of_ref, ob_ref, z_ref, mod_ref, gnw_ref, n2w_ref,
                  wg_ref, wb_ref, wo_ref, wr_ref, xo_ref, h2_ref, aff_ref, afft_ref, *, nb, ctx_tiles, d):
    m = _mod_row(mod_ref, nb, ctx_tiles)
    g1, sh2, sc2 = m[:, 2 * d:3 * d], m[:, 3 * d:4 * d], m[:, 4 * d:5 * d]
    o = of_ref[0] + ob_ref[0]
    z = z_ref[0].astype(F32)
    ygs = []
    for h in range(GDN_HEADS):
        sl = slice(h * GDN_DK, (h + 1) * GDN_DK)
        ygs.append(_rms(o[:, sl], gnw_ref[...]) * _silu(z[:, sl]))
    yg = jnp.concatenate(ygs, axis=1).astype(BF16)
    hb = h_ref[0]
    merged = jnp.zeros((TM, d), F32)
    is_ctx = pl.program_id(1) < ctx_tiles
    ya = jnp.where(is_ctx, yac_ref[0], yal_ref[0])
    yb = jnp.where(is_ctx, ybc_ref[0], ybl_ref[0])
    for i, y in enumerate((ya, yb, yg)):
        gate = _sigmoid(jnp.dot(hb, wg_ref[i], preferred_element_type=F32))
        merged = merged + gate * jnp.dot(y, wb_ref[i], preferred_element_type=F32)
    x = x_ref[0] + g1 * jnp.dot(merged.astype(BF16), wo_ref[...], preferred_element_type=F32)
    xo_ref[0] = x
    h2 = _rms(x, n2w_ref[...]) * (1.0 + sc2) + sh2
    h2b = h2.astype(BF16)
    h2_ref[0] = h2b
    logits = _mm3(h2, wr_ref[...])
    lane = lax.broadcasted_iota(jnp.int32, logits.shape, 1)
    logits = jnp.where(lane < N_EXPERTS, logits, -jnp.inf)
    e = jnp.exp(logits - jnp.max(logits, axis=-1, keepdims=True))
    aff = e / jnp.sum(e, axis=-1, keepdims=True)
    aff_ref[0] = aff
    afft_ref[0] = aff.T[:N_EXPERTS]


def _merge(xs, h, ya, yb, o_f, o_b, z, mod, gnw, n2w, wg, wb, wo, wr, *, nb, ctx_tiles):
    b, t, d = xs.shape
    nt = t // TM
    tile = lambda width: pl.BlockSpec((1, TM, width), lambda i, j: (i, j, 0))
    whole = lambda a: pl.BlockSpec(a.shape, lambda i, j: (0,) * a.ndim)
    (yac, yal), (ybc, ybl) = ya, yb
    ctx_tile = pl.BlockSpec((1, TM, 512), lambda i, j: (i, jnp.minimum(j, ctx_tiles - 1), 0))
    lat_tile = pl.BlockSpec((1, TM, 512), lambda i, j: (i, jnp.maximum(j - ctx_tiles, 0), 0))
    return pl.pallas_call(
        functools.partial(_merge_kernel, nb=nb, ctx_tiles=ctx_tiles, d=d),
        grid=(b, nt),
        in_specs=[tile(d), tile(d), ctx_tile, lat_tile, ctx_tile, lat_tile, tile(512), tile(512), tile(512),
                  whole(mod), whole(gnw), whole(n2w), whole(wg), whole(wb), whole(wo), whole(wr)],
        out_specs=(tile(d), tile(d), tile(LANES), pl.BlockSpec((1, N_EXPERTS, TM), lambda i, j: (i, 0, j))),
        out_shape=(jax.ShapeDtypeStruct((b, t, d), F32), jax.ShapeDtypeStruct((b, t, d), BF16),
                   jax.ShapeDtypeStruct((b, t, LANES), F32), jax.ShapeDtypeStruct((b, N_EXPERTS, t), F32)),
        compiler_params=_cparams(("parallel", "parallel")),
        name="merge",
    )(xs, h, yac, yal, ybc, ybl, o_f, o_b, z, mod, gnw, n2w, wg, wb, wo, wr)


ROW_ALIGN = 16
SLOT_GROUP = 64
FFN_CHUNKS = 3


def _moe_rows(caps, nt):
    chunk = -(-(sum(caps) + (ROW_ALIGN - 1) * nt) // (ROW_ALIGN * FFN_CHUNKS)) * ROW_ALIGN
    return chunk, FFN_CHUNKS * chunk + TM


def _route_kernel(afft_ref, post_ref, posc_ref, starts_ref, *, ctx_tiles, nt, cap_ctx, cap_lat):
    a = afft_ref[0]
    bits = pltpu.bitcast(a, jnp.int32)
    lane = lax.broadcasted_iota(jnp.int32, a.shape, 1)
    isctx = lane < ctx_tiles * TM

    def bisect(i, carry):
        tc, tl = carry
        bit = jnp.left_shift(jnp.int32(1), 30 - i)
        cc, cl = tc | bit, tl | bit
        ge = bits >= jnp.where(isctx, cc, cl)
        n_c = jnp.sum(jnp.where(jnp.logical_and(ge, isctx), 1.0, 0.0), axis=-1, keepdims=True)
        n_l = jnp.sum(jnp.where(jnp.logical_and(ge, jnp.logical_not(isctx)), 1.0, 0.0), axis=-1, keepdims=True)
        return jnp.where(n_c >= cap_ctx, cc, tc), jnp.where(n_l >= cap_lat, cl, tl)

    zero = jnp.zeros((N_EXPERTS, 1), jnp.int32)
    tc, tl = lax.fori_loop(0, 31, bisect, (zero, zero))
    thr = jnp.where(isctx, tc, tl)
    gt = bits > thr
    eq = bits == thr
    n_gt_c = jnp.sum(jnp.where(jnp.logical_and(gt, isctx), 1.0, 0.0), axis=-1, keepdims=True)
    n_gt_l = jnp.sum(jnp.where(jnp.logical_and(gt, jnp.logical_not(isctx)), 1.0, 0.0), axis=-1, keepdims=True)
    r = lax.broadcasted_iota(jnp.int32, (TM, TM), 0)
    c = lax.broadcasted_iota(jnp.int32, (TM, TM), 1)
    upper = jnp.where(r < c, 1.0, 0.0).astype(BF16)
    lane_t = lax.broadcasted_iota(jnp.int32, (N_EXPERTS, LANES), 1)
    starts = jnp.zeros((N_EXPERTS, LANES), F32)
    start = jnp.zeros((N_EXPERTS, 1), F32)
    eq_seen = jnp.zeros((N_EXPERTS, 1), F32)
    for j in range(nt):
        sl = slice(j * TM, (j + 1) * TM)
        if j == ctx_tiles:
            eq_seen = jnp.zeros((N_EXPERTS, 1), F32)
        need = (cap_ctx - n_gt_c) if j < ctx_tiles else (cap_lat - n_gt_l)
        eq_j = jnp.where(eq[:, sl], 1.0, 0.0)
        rank = eq_seen + _mm(eq_j, upper)
        eq_seen = eq_seen + jnp.sum(eq_j, axis=-1, keepdims=True)
        sel = jnp.logical_or(gt[:, sl], jnp.logical_and(eq[:, sl], rank < need))
        sel_f = jnp.where(sel, 1.0, 0.0)
        pos = jnp.where(sel, start + _mm(sel_f, upper), -1.0)
        post_ref[0, :, sl] = pos
        posc_ref[0, sl, :] = jnp.concatenate([pos, jnp.full((LANES - N_EXPERTS, TM), -1.0, F32)], axis=0).T
        starts = jnp.where(lane_t == j, start, starts)
        cnt = jnp.sum(sel_f, axis=-1, keepdims=True)
        start = start + jnp.floor((cnt + (ROW_ALIGN - 1)) * (1.0 / ROW_ALIGN)) * ROW_ALIGN
    starts_ref[0] = jnp.where(lane_t == nt, start, starts).astype(jnp.int32)


def _route(afft, *, ctx_tiles, cap_ctx, cap_lat):
    b, e, t = afft.shape
    nt = t // TM
    return pl.pallas_call(
        functools.partial(_route_kernel, ctx_tiles=ctx_tiles, nt=nt, cap_ctx=cap_ctx, cap_lat=cap_lat),
        grid=(b,),
        in_specs=[pl.BlockSpec((1, e, t), lambda i: (i, 0, 0))],
        out_specs=(pl.BlockSpec((1, e, t), lambda i: (i, 0, 0)),
                   pl.BlockSpec((1, t, LANES), lambda i: (i, 0, 0)),
                   pl.BlockSpec((1, e, LANES), lambda i: (i, 0, 0))),
        out_shape=(jax.ShapeDtypeStruct((b, e, t), F32), jax.ShapeDtypeStruct((b, t, LANES), F32),
                   jax.ShapeDtypeStruct((b, e, LANES), jnp.int32)),
        compiler_params=_cparams(("parallel",)),
        name="moe_route",
    )(afft)


def _moe_ffn_kernel(starts_ref, h2_ref, post_ref, wg_ref, wu_ref, wd_ref, y_ref, xin_ref, *, nt, unroll, chunk, n_picks):
    b, e = pl.program_id(0), pl.program_id(1)
    xin_ref[n_picks:, :] = jnp.zeros((xin_ref.shape[0] - n_picks, xin_ref.shape[1]), F32)
    slot = lax.broadcasted_iota(jnp.int32, (SLOT_GROUP, TM), 0).astype(F32)

    def picks(j, g):
        base = (b * N_EXPERTS + e) * LANES + j
        off = starts_ref[base]
        tok = pl.multiple_of(j * TM, TM)
        pos = post_ref[0, pl.ds(e, 1), pl.ds(tok, TM)] - (off + g * SLOT_GROUP).astype(F32)
        onehot = jnp.where(pos == slot, 1.0, 0.0).astype(BF16)
        blk = jnp.dot(onehot, h2_ref[0, pl.ds(tok, TM), :], preferred_element_type=F32)
        return pl.multiple_of(off + g * SLOT_GROUP, ROW_ALIGN), starts_ref[base + 1] - off - g * SLOT_GROUP, blk

    def first_groups(i, carry):
        blocks = [picks(i * unroll + k, 0) for k in range(unroll)]
        for row0, _, blk in blocks:
            xin_ref[pl.ds(row0, SLOT_GROUP), :] = blk
        return carry

    lax.fori_loop(0, nt // unroll, first_groups, 0)

    def more_groups(j, carry):
        def group(g, carry):
            row0, n_real, blk = picks(j, g)
            keep = lax.broadcasted_iota(jnp.int32, blk.shape, 0) < n_real
            xin_ref[pl.ds(row0, SLOT_GROUP), :] = jnp.where(keep, blk, xin_ref[pl.ds(row0, SLOT_GROUP), :])
            return carry

        base = (b * N_EXPERTS + e) * LANES + j
        n_groups = (starts_ref[base + 1] - starts_ref[base] + (SLOT_GROUP - 1)) // SLOT_GROUP
        return lax.fori_loop(1, n_groups, group, carry)

    lax.fori_loop(0, nt, more_groups, 0)

    def ffn(c, carry):
        rows = pl.ds(pl.multiple_of(c * chunk, ROW_ALIGN), chunk)
        x = xin_ref[rows, :].astype(BF16)
        hid = _silu(jnp.dot(x, wg_ref[0], preferred_element_type=F32)) * jnp.dot(x, wu_ref[0], preferred_element_type=F32)
        y_ref[0, 0, rows, :] = jnp.dot(hid.astype(BF16), wd_ref[0], preferred_element_type=F32).astype(BF16)
        return carry

    lax.fori_loop(0, FFN_CHUNKS, ffn, 0)
    y_ref[0, 0, FFN_CHUNKS * chunk:, :] = jnp.zeros((TM, y_ref.shape[3]), BF16)


def _moe_ffn(starts, h2, post, wg, wu, wd, *, chunk, rows, n_picks):
    b, t, d = h2.shape
    nt = t // TM
    e, _, ff = wg.shape
    grid_spec = pltpu.PrefetchScalarGridSpec(
        num_scalar_prefetch=1,
        grid=(b, e),
        in_specs=[pl.BlockSpec((1, t, d), lambda i, k, s: (i, 0, 0), pipeline_mode=pl.Buffered(1)),
                  pl.BlockSpec((1, e, t), lambda i, k, s: (i, 0, 0)),
                  pl.BlockSpec((1, d, ff), lambda i, k, s: (k, 0, 0)),
                  pl.BlockSpec((1, d, ff), lambda i, k, s: (k, 0, 0)),
                  pl.BlockSpec((1, ff, d), lambda i, k, s: (k, 0, 0))],
        out_specs=pl.BlockSpec((1, 1, rows, d), lambda i, k, s: (i, k, 0, 0)),
        scratch_shapes=[pltpu.VMEM((rows, d), F32)],
    )
    return pl.pallas_call(
        functools.partial(_moe_ffn_kernel, nt=nt, unroll=max(u for u in range(1, 12) if nt % u == 0), chunk=chunk,
                          n_picks=n_picks),
        grid_spec=grid_spec,
        out_shape=jax.ShapeDtypeStruct((b, e, rows, d), BF16),
        compiler_params=pltpu.CompilerParams(dimension_semantics=("parallel", "arbitrary"),
                                             vmem_limit_bytes=MOE_FFN_VMEM_LIMIT),
        name="moe_ffn",
    )(starts, h2, post, wg, wu, wd)


def _moe_combine_kernel(starts_ref, y_ref, posc_ref, aff_ref, x_ref, g2_ref, o_ref, *, nb, ctx_tiles):
    b, j = pl.program_id(0), pl.program_id(2)
    slot = lax.broadcasted_iota(jnp.int32, (TM, TM), 1).astype(F32)
    posc, aff = posc_ref[0], aff_ref[0]
    acc = jnp.zeros(o_ref.shape[1:], F32)
    for e in range(N_EXPERTS):
        off = pl.multiple_of(starts_ref[(b * N_EXPERTS + e) * LANES + j], ROW_ALIGN)
        pos = posc[:, e:e + 1]
        onehot = jnp.where(pos - off.astype(F32) == slot, 1.0, 0.0).astype(BF16)
        w = jnp.where(pos >= 0.0, aff[:, e:e + 1], 0.0)
        acc = acc + w * jnp.dot(onehot, y_ref[0, e, pl.ds(off, TM), :], preferred_element_type=F32)
    row = jnp.where(j < ctx_tiles, nb, b)
    o_ref[0] = x_ref[0] + g2_ref[pl.ds(row, 1), :] * acc


def _moe_combine(starts, y, posc, aff, xs, g2, *, nb, ctx_tiles, dsplit=2):
    b, t, d = xs.shape
    e, rows = y.shape[1:3]
    dq = d // dsplit
    grid_spec = pltpu.PrefetchScalarGridSpec(
        num_scalar_prefetch=1,
        grid=(b, dsplit, t // TM),
        in_specs=[pl.BlockSpec((1, e, rows, dq), lambda i, q, j, s: (i, 0, 0, q), pipeline_mode=pl.Buffered(1)),
                  pl.BlockSpec((1, TM, LANES), lambda i, q, j, s: (i, j, 0)),
                  pl.BlockSpec((1, TM, LANES), lambda i, q, j, s: (i, j, 0)),
                  pl.BlockSpec((1, TM, dq), lambda i, q, j, s: (i, j, q)),
                  pl.BlockSpec((MOD_ROWS, dq), lambda i, q, j, s: (0, q))],
        out_specs=pl.BlockSpec((1, TM, dq), lambda i, q, j, s: (i, j, q)),
    )
    return pl.pallas_call(
        functools.partial(_moe_combine_kernel, nb=nb, ctx_tiles=ctx_tiles),
        grid_spec=grid_spec,
        out_shape=jax.ShapeDtypeStruct((b, t, d), F32),
        compiler_params=_cparams(("parallel", "parallel", "parallel")),
        name="moe_combine",
    )(starts, y, posc, aff, xs, g2)


def _lambda_init(layer_idx):
    return 0.8 - 0.6 * math.exp(-0.3 * layer_idx)


def _rope_tables(n_lat, n_ctx):
    rows = n_lat // GRID_W
    row = jnp.repeat(jnp.arange(rows, dtype=F32), GRID_W)
    col = jnp.tile(jnp.arange(GRID_W, dtype=F32), rows)
    axis_dim = HEAD_DIM // 2
    inv_freq = ROPE_THETA ** (-jnp.arange(0, axis_dim, 2, dtype=F32) / axis_dim)
    ang_r, ang_c = row[:, None] * inv_freq, col[:, None] * inv_freq
    cos = jnp.concatenate([jnp.cos(ang_r)] * 2 + [jnp.cos(ang_c)] * 2, axis=-1)
    sin = jnp.concatenate([-jnp.sin(ang_r), jnp.sin(ang_r), -jnp.sin(ang_c), jnp.sin(ang_c)], axis=-1)
    cos = jnp.concatenate([jnp.ones((n_ctx, HEAD_DIM), F32), cos], axis=0)
    sin = jnp.concatenate([jnp.zeros((n_ctx, HEAD_DIM), F32), sin], axis=0)
    return jnp.tile(cos, (1, 2)), jnp.tile(sin, (1, 2))


def kernel(x, c, ctx, c_ctx, w_mod, b_mod, norm1_w, norm2_w, w_in, gqa_q_norm, gqa_k_norm, diff_q_norm,
           diff_k_norm, diff_lambda_q1, diff_lambda_k1, diff_lambda_q2, diff_lambda_k2, diff_subln,
           gdn_conv_w, gdn_a_log, gdn_dt_bias, gdn_norm_w, w_merge_gate, w_branch, w_out, w_router,
           w_exp_gate, w_exp_up, w_exp_down):
    nb, n_lat, d = x.shape
    n_ctx = ctx.shape[1]
    depth = w_mod.shape[0]
    assert n_ctx % TM == 0 and n_lat % TM == 0 and nb < MOD_ROWS and d == 1024
    ctx_tiles = n_ctx // TM
    ctx_chunks = n_ctx // GDN_CHUNK

    cos_t, sin_t = _rope_tables(n_lat, n_ctx)
    cc = jnp.zeros((MOD_ROWS, d), F32).at[:nb].set(c).at[nb].set(c_ctx)
    mods = _modulation(cc, w_mod, b_mod)
    half = lax.broadcasted_iota(jnp.int32, (LANES, LANES), 0) // HEAD_DIM
    bd = (half == half.T).astype(BF16)

    xs = jnp.concatenate([ctx, x], axis=1)
    scale = HEAD_DIM ** -0.5 * math.log2(math.e)
    for li in range(depth):
        lam_init = _lambda_init(li)
        w_in_p = jnp.pad(w_in[li], ((0, 0), (0, 4480 - w_in.shape[-1]))).astype(BF16)
        qkn = jnp.stack([jnp.tile(gqa_q_norm[li] * scale, 2), jnp.tile(gqa_k_norm[li], 2),
                         jnp.tile(diff_q_norm[li] * scale, 2), jnp.tile(diff_k_norm[li], 2)])
        (h, qa, ka, va, qb, kb, vb, cqkv, z, gates) = _inproj(
            xs, mods[li], norm1_w[li][None, :], w_in_p, cos_t, sin_t, qkn, bd, nb=nb, ctx_tiles=ctx_tiles)

        def score_bound(qn, kn):
            return HEAD_DIM * jnp.max(jnp.abs(qn * scale)) * jnp.max(jnp.abs(kn))

        ya = lax.cond(score_bound(gqa_q_norm[li], gqa_k_norm[li]) <= SCORE_LIMIT,
                      functools.partial(_gqa_attention, n_ctx=n_ctx, bounded=True),
                      functools.partial(_gqa_attention, n_ctx=n_ctx, bounded=False), qa, ka, va)
        lam_rows = jnp.stack([jnp.concatenate([diff_lambda_q1[li], diff_lambda_q2[li]]),
                              jnp.concatenate([diff_lambda_k1[li], diff_lambda_k2[li]])])
        yb = lax.cond(score_bound(diff_q_norm[li], diff_k_norm[li]) <= SCORE_LIMIT,
                      functools.partial(_diff_attention, n_ctx=n_ctx, lam_init=lam_init, bounded=True),
                      functools.partial(_diff_attention, n_ctx=n_ctx, lam_init=lam_init, bounded=False),
                      qb, kb, vb, lam_rows, diff_subln[li][None, :])

        conv_w8 = jnp.pad(gdn_conv_w[li], ((0, 5), (0, 0)))
        gparams = jnp.zeros((2, LANES), F32)
        gparams = gparams.at[0, 8:16].set(gdn_a_log[li].reshape(-1)).at[1, 8:16].set(gdn_dt_bias[li].reshape(-1))
        qg, kg, vg, bg = _gdn_prep(cqkv, gates, conv_w8, gparams, ctx_tiles=ctx_tiles)
        l1, l2, u, gl = _gdn_chunks(qg, kg, vg, bg)
        o_f, o_b = _gdn_scan(l1, l2, u, gl, ctx_chunks=ctx_chunks)

        wr = jnp.pad(w_router[li], ((0, 0), (0, LANES - N_EXPERTS)))
        xs, h2, aff, afft = _merge(xs, h, ya, yb, o_f, o_b, z, mods[li], gdn_norm_w[li][None, :],
                                   norm2_w[li][None, :], w_merge_gate[li].astype(BF16), w_branch[li].astype(BF16),
                                   w_out[li].astype(BF16), wr, nb=nb, ctx_tiles=ctx_tiles)

        caps = (CAPACITY_FACTOR * n_ctx // N_EXPERTS, CAPACITY_FACTOR * n_lat // N_EXPERTS)
        chunk, rows = _moe_rows(caps, (n_ctx + n_lat) // TM)
        post, posc, starts = _route(afft, ctx_tiles=ctx_tiles, cap_ctx=caps[0], cap_lat=caps[1])
        starts = starts.reshape(-1)
        y = _moe_ffn(starts, h2, post, w_exp_gate[li].astype(BF16), w_exp_up[li].astype(BF16),
                     w_exp_down[li].astype(BF16), chunk=chunk, rows=rows, n_picks=sum(caps))
        xs = _moe_combine(starts, y, posc, aff, xs, mods[li, :, 5 * d:], nb=nb, ctx_tiles=ctx_tiles)
    return xs[:, n_ctx:]
```

```python
import functools
import math

import jax
import jax.numpy as jnp
from jax import lax
from jax.experimental import pallas as pl
from jax.experimental.pallas import tpu as pltpu

F32 = jnp.float32
BF16 = jnp.bfloat16

LANES = 128
TM = 256
GRID_W = 64
HEAD_DIM = 64
GQA_HEADS = 8
GQA_KV_HEADS = 2
DIFF_HEADS = 4
GDN_HEADS = 4
GDN_DK = 128
GDN_CHUNK = 64
N_EXPERTS = 16
CAPACITY_FACTOR = 2
ROPE_THETA = 10000.0
EPS = 1e-6
ONES_ROWS = 16
MOD_ROWS = 16
VMEM_LIMIT = 56 * 1024 * 1024
MOE_FFN_VMEM_LIMIT = 60 * 1024 * 1024


def _cparams(sem):
    return pltpu.CompilerParams(dimension_semantics=sem, vmem_limit_bytes=VMEM_LIMIT)


def _mm(a, b):
    return jnp.dot(a.astype(BF16), b.astype(BF16), preferred_element_type=F32)


def _mm_nt(a, b):
    return lax.dot_general(a.astype(BF16), b.astype(BF16), (((1,), (1,)), ((), ())),
                           preferred_element_type=F32)


def _split2(a):
    hi = a.astype(BF16)
    lo = (a - hi.astype(F32)).astype(BF16)
    return hi, lo


def _mm3(a, b):
    ah, al = _split2(a)
    bh, bl = _split2(b)
    return _mm(ah, bh) + _mm(ah, bl) + _mm(al, bh)


def _mm_exact_lhs(a_bf16, b):
    b1 = b.astype(BF16)
    r1 = b - b1.astype(F32)
    b2 = r1.astype(BF16)
    b3 = (r1 - b2.astype(F32)).astype(BF16)
    return _mm(a_bf16, b1) + _mm(a_bf16, b2) + _mm(a_bf16, b3)


def _sigmoid(x):
    return 1.0 / (1.0 + jnp.exp(-x))


def _silu(x):
    return x * _sigmoid(x)


def _mod_kernel(a_ref, w_ref, b_ref, o_ref):
    a = _silu(a_ref[...])
    o_ref[0] = _mm3(a, w_ref[0]) + b_ref[0]


def _modulation(cc, w_mod, b_mod):
    depth, d, d6 = w_mod.shape
    nj = d6 // d
    return pl.pallas_call(
        _mod_kernel,
        grid=(depth, nj),
        in_specs=[pl.BlockSpec((MOD_ROWS, d), lambda l, j: (0, 0)),
                  pl.BlockSpec((1, d, d), lambda l, j: (l, 0, j)),
                  pl.BlockSpec((1, 1, d), lambda l, j: (l, 0, j))],
        out_specs=pl.BlockSpec((1, MOD_ROWS, d), lambda l, j: (l, 0, j)),
        out_shape=jax.ShapeDtypeStruct((depth, MOD_ROWS, d6), F32),
        compiler_params=_cparams(("parallel", "parallel")),
        name="modulation",
    )(cc, w_mod, b_mod.reshape(depth, 1, d6))


def _mod_row(mod_ref, nb, ctx_tiles):
    b = pl.program_id(0)
    t = pl.program_id(1)
    row = jnp.where(t < ctx_tiles, nb, b)
    return mod_ref[pl.ds(row, 1), :]


def _rms(x, w):
    return x * lax.rsqrt(jnp.mean(x * x, axis=-1, keepdims=True) + EPS) * w


def _half_norm_rope(t, w, bd, cos, sin):
    ssq = _mm(t * t, bd)
    tn = t * lax.rsqrt(ssq * (1.0 / HEAD_DIM) + EPS) * w
    lane = lax.broadcasted_iota(jnp.int32, tn.shape, 1)
    partner = jnp.where((lane % 32) < 16, pltpu.roll(tn, LANES - 16, axis=1), pltpu.roll(tn, 16, axis=1))
    return tn * cos + partner * sin


def _inproj_kernel(x_ref, mod_ref, nw_ref, w_ref, cos_ref, sin_ref, qkn_ref, bd_ref,
                   h_ref, qa_ref, ka_ref, va_ref, qb_ref, kb_ref, vb_ref, c_ref, z_ref, g_ref,
                   *, nb, ctx_tiles, d):
    m = _mod_row(mod_ref, nb, ctx_tiles)
    sh, sc = m[:, 0:d], m[:, d:2 * d]
    h = _rms(x_ref[0], nw_ref[...]) * (1.0 + sc) + sh
    hb = h.astype(BF16)
    h_ref[0] = hb
    cos, sin, bd = cos_ref[...], sin_ref[...], bd_ref[...]

    def proj(lo, width):
        return jnp.dot(hb, w_ref[:, lo:lo + width], preferred_element_type=F32)

    ones_rows = jnp.ones((ONES_ROWS, TM), BF16)
    group = lambda t, g: t[:, g * LANES:(g + 1) * LANES]
    q_all, kv = proj(0, 512), proj(512, 2 * LANES)
    for g in range(4):
        t = _half_norm_rope(group(q_all, g), qkn_ref[0:1, :], bd, cos, sin)
        qa_ref[0, g * LANES:(g + 1) * LANES, :] = t.T.astype(BF16)
    ka_ref[0] = _half_norm_rope(group(kv, 0), qkn_ref[1:2, :], bd, cos, sin).astype(BF16)
    vt = group(kv, 1).T.astype(BF16)
    for j in range(GQA_KV_HEADS):
        va_ref[0, j, 0:HEAD_DIM, :] = vt[j * HEAD_DIM:(j + 1) * HEAD_DIM]
        va_ref[0, j, HEAD_DIM:HEAD_DIM + ONES_ROWS, :] = ones_rows
    q_all, k_all, v_all = proj(768, 512), proj(1280, 512), proj(1792, 512)
    for g in range(4):
        t = _half_norm_rope(group(q_all, g), qkn_ref[2:3, :], bd, cos, sin)
        qb_ref[0, g * LANES:(g + 1) * LANES, :] = t.T.astype(BF16)
        kb_ref[0, :, g * LANES:(g + 1) * LANES] = _half_norm_rope(group(k_all, g), qkn_ref[3:4, :], bd, cos,
                                                                  sin).astype(BF16)
        vb_ref[0, g, 0:LANES, :] = group(v_all, g).T.astype(BF16)
        vb_ref[0, g, LANES:LANES + ONES_ROWS, :] = ones_rows
    for g in range(3):
        c_ref[0, :, g * 512:(g + 1) * 512] = proj(2304 + g * 512, 512)
    z_ref[0] = proj(3840, 512).astype(BF16)
    g_ref[0] = proj(4352, LANES)


def _inproj(xs, mod, nw, w_in_p, cos_t, sin_t, qkn, bd, *, nb, ctx_tiles):
    b, t, d = xs.shape
    nt = t // TM
    tile = lambda width: pl.BlockSpec((1, TM, width), lambda i, j: (i, j, 0))
    whole = lambda a: pl.BlockSpec(a.shape, lambda i, j: (0,) * a.ndim)
    out_shape = (
        jax.ShapeDtypeStruct((b, t, d), BF16),
        jax.ShapeDtypeStruct((b, 512, t), BF16),
        jax.ShapeDtypeStruct((b, t, LANES), BF16),
        jax.ShapeDtypeStruct((b, 2, HEAD_DIM + ONES_ROWS, t), BF16),
        jax.ShapeDtypeStruct((b, 512, t), BF16),
        jax.ShapeDtypeStruct((b, t, 512), BF16),
        jax.ShapeDtypeStruct((b, 4, LANES + ONES_ROWS, t), BF16),
        jax.ShapeDtypeStruct((b, t, 1536), F32),
        jax.ShapeDtypeStruct((b, t, 512), BF16),
        jax.ShapeDtypeStruct((b, t, LANES), F32),
    )
    out_specs = (
        tile(d),
        pl.BlockSpec((1, 512, TM), lambda i, j: (i, 0, j)),
        tile(LANES),
        pl.BlockSpec((1, 2, HEAD_DIM + ONES_ROWS, TM), lambda i, j: (i, 0, 0, j)),
        pl.BlockSpec((1, 512, TM), lambda i, j: (i, 0, j)),
        tile(512),
        pl.BlockSpec((1, 4, LANES + ONES_ROWS, TM), lambda i, j: (i, 0, 0, j)),
        tile(1536), tile(512), tile(LANES),
    )
    return pl.pallas_call(
        functools.partial(_inproj_kernel, nb=nb, ctx_tiles=ctx_tiles, d=d),
        grid=(b, nt),
        in_specs=[tile(d), whole(mod), whole(nw), whole(w_in_p),
                  pl.BlockSpec((TM, LANES), lambda i, j: (j, 0)),
                  pl.BlockSpec((TM, LANES), lambda i, j: (j, 0)),
                  whole(qkn), whole(bd)],
        out_specs=out_specs,
        out_shape=out_shape,
        compiler_params=_cparams(("parallel", "parallel")),
        name="inproj",
    )(xs, mod, nw, w_in_p, cos_t, sin_t, qkn, bd)


def _softmax_step(st, vt, mx, acc):
    mnew = jnp.maximum(mx, jnp.max(st, axis=0, keepdims=True))
    p = jnp.exp2(st - mnew).astype(BF16)
    acc = acc * jnp.exp2(mx - mnew) + jnp.dot(vt, p, preferred_element_type=F32)
    return mnew, acc


def _attend_bounded(w, k_at, vt_at, n_ctx, n_lat, vrows):
    def step(off, n, acc):
        p = jnp.exp2(jnp.dot(k_at(off, n), w, preferred_element_type=F32)).astype(BF16)
        return acc + jnp.dot(vt_at(off, n), p, preferred_element_type=F32)

    acc = step(0, n_ctx, jnp.zeros((vrows, w.shape[1]), F32))
    tk = min(BOUNDED_TK, n_lat) if n_lat else 0
    for c in range(n_lat // tk if tk else 0):
        acc = step(n_ctx + c * tk, tk, acc)
    return acc


def _attend(w, k_at, vt_at, n_ctx, n_groups, tk, s_refs, vrows):
    unroll = len(s_refs)

    def scores(c):
        return jnp.dot(k_at(lat(c), tk), w, preferred_element_type=F32)

    def lat(c):
        return pl.multiple_of(n_ctx + c * tk, TM)

    mx = jnp.full((1, w.shape[1]), -jnp.inf, F32)
    acc = jnp.zeros((vrows, w.shape[1]), F32)
    mx, acc = _softmax_step(jnp.dot(k_at(0, n_ctx), w, preferred_element_type=F32), vt_at(0, n_ctx), mx, acc)
    if n_groups == 0:
        return acc

    def group(i, carry, prefetch):
        mx, acc = carry
        for u in range(unroll):
            c = i * unroll + u
            if u + 1 < unroll or prefetch:
                s_refs[(u + 1) % unroll][...] = scores(c + 1)
            mx, acc = _softmax_step(s_refs[u][...], vt_at(lat(c), tk), mx, acc)
        return mx, acc

    s_refs[0][...] = scores(0)
    carry = lax.fori_loop(0, n_groups - 1, functools.partial(group, prefetch=True), (mx, acc))
    return group(n_groups - 1, carry, False)[1]


SCORE_LIMIT = 64.0
BOUNDED_TK = 1024
ATTN_Q_BLOCKS = 2
ATTN_UNROLL = 4
ATTN_TK = 512


def _attn_chunk(n_lat):
    tk = min(ATTN_TK, max(TM, n_lat // ATTN_UNROLL))
    unroll = min(ATTN_UNROLL, n_lat // tk)
    assert n_lat % (unroll * tk) == 0 and tk % TM == 0 and n_lat % (ATTN_Q_BLOCKS * TM) == 0
    return tk, unroll, n_lat // (unroll * tk)


def _attention_call(kern, name, q_t, others, other_specs, *, n_ctx, heads, bounded):
    b, _, t = q_t.shape
    n_lat = t - n_ctx
    tk, unroll, n_groups = _attn_chunk(n_lat)
    nq, ctx_tiles = ATTN_Q_BLOCKS, n_ctx // TM
    tq = nq * TM
    q_specs = [pl.BlockSpec((1, LANES, TM), functools.partial(lambda i, g, j, u: (i, g, ctx_tiles + nq * j + u), u=u))
               for u in range(nq)]
    lat = pl.pallas_call(
        functools.partial(kern, nq=nq, n_ctx=n_ctx, n_lat=n_lat, n_groups=n_groups, tk=tk, bounded=bounded),
        grid=(b, heads, n_lat // tq),
        in_specs=q_specs + other_specs(t),
        out_specs=pl.BlockSpec((1, tq, LANES), lambda i, g, j: (i, j, g)),
        out_shape=jax.ShapeDtypeStruct((b, n_lat, 512), BF16),
        scratch_shapes=[] if bounded else [pltpu.VMEM((tk, 2 * tq), F32)] * unroll,
        compiler_params=_cparams(("parallel", "parallel", "parallel")),
        name=name,
    )(*([q_t] * nq), *others)
    ctx = pl.pallas_call(
        functools.partial(kern, nq=1, n_ctx=n_ctx, n_lat=0, n_groups=0, tk=tk, bounded=bounded),
        grid=(b, heads, ctx_tiles),
        in_specs=[pl.BlockSpec((1, LANES, TM), lambda i, g, j: (i, g, j))] + other_specs(n_ctx),
        out_specs=pl.BlockSpec((1, TM, LANES), lambda i, g, j: (i, j, g)),
        out_shape=jax.ShapeDtypeStruct((b, n_ctx, 512), BF16),
        compiler_params=_cparams(("parallel", "parallel", "parallel")),
        name=name + "_ctx",
    )(q_t, *others)
    return ctx, lat


def _gqa_kernel(*refs, nq, n_ctx, n_lat, n_groups, tk, bounded):
    qt_refs, (k_ref, vt_ref, o_ref), s_refs = refs[:nq], refs[nq:nq + 3], refs[nq + 3:]
    tq = nq * TM
    top = jnp.concatenate([r[0, :HEAD_DIM] for r in qt_refs] + [r[0, HEAD_DIM:] for r in qt_refs], axis=1)
    zero = jnp.zeros_like(top)
    w = jnp.where(pl.program_id(1) < 2, jnp.concatenate([top, zero], axis=0), jnp.concatenate([zero, top], axis=0))
    vrows = HEAD_DIM + ONES_ROWS
    k_at, vt_at = (lambda off, n: k_ref[0, pl.ds(off, n), :]), (lambda off, n: vt_ref[0, 0, :, pl.ds(off, n)])
    if bounded:
        acc = _attend_bounded(w, k_at, vt_at, n_ctx, n_lat, vrows)
    else:
        acc = _attend(w, k_at, vt_at, n_ctx, n_groups, tk, s_refs, vrows)
    o = acc[:HEAD_DIM] / acc[HEAD_DIM:HEAD_DIM + 1]
    o_ref[0] = jnp.concatenate([o[:, :tq], o[:, tq:]], axis=0).T.astype(BF16)


def _gqa_attention(qat, ka, vat, *, n_ctx, bounded):
    vrows = vat.shape[2]
    specs = lambda keys: [pl.BlockSpec((1, keys, LANES), lambda i, g, j: (i, 0, 0)),
                          pl.BlockSpec((1, 1, vrows, keys), lambda i, g, j: (i, g // 2, 0, 0))]
    return _attention_call(_gqa_kernel, "gqa_attention", qat, (ka, vat), specs, n_ctx=n_ctx, heads=4, bounded=bounded)


def _diff_kernel(*refs, nq, n_ctx, n_lat, n_groups, tk, bounded, lam_init):
    qt_refs, (k_ref, vt_ref, lam_ref, sw_ref, o_ref), s_refs = refs[:nq], refs[nq:nq + 5], refs[nq + 5:]
    tq = nq * TM
    qt = jnp.concatenate([r[0] for r in qt_refs], axis=1)
    zero = jnp.zeros((HEAD_DIM, tq), BF16)
    w = jnp.concatenate([jnp.concatenate([qt[:HEAD_DIM], zero], axis=0),
                         jnp.concatenate([zero, qt[HEAD_DIM:]], axis=0)], axis=1)
    vrows = LANES + ONES_ROWS
    k_at, vt_at = (lambda off, n: k_ref[0, pl.ds(off, n), :]), (lambda off, n: vt_ref[0, 0, :, pl.ds(off, n)])
    if bounded:
        acc = _attend_bounded(w, k_at, vt_at, n_ctx, n_lat, vrows)
    else:
        acc = _attend(w, k_at, vt_at, n_ctx, n_groups, tk, s_refs, vrows)
    prod = lam_ref[0:1, :] * lam_ref[1:2, :]
    low = lax.broadcasted_iota(jnp.int32, prod.shape, 1) < HEAD_DIM
    l1 = jnp.sum(jnp.where(low, prod, 0.0), axis=-1, keepdims=True)
    l2 = jnp.sum(jnp.where(low, 0.0, prod), axis=-1, keepdims=True)
    lam = jnp.exp(l1) - jnp.exp(l2) + lam_init
    o1 = acc[:LANES, :tq] / acc[LANES:LANES + 1, :tq]
    o2 = acc[:LANES, tq:] / acc[LANES:LANES + 1, tq:]
    o = (o1 - lam * o2).T
    o_ref[0] = (_rms(o, sw_ref[...]) * (1.0 - lam_init)).astype(BF16)


def _diff_attention(qbt, kb, vbt, lam_rows, subln, *, n_ctx, lam_init, bounded):
    vrows = vbt.shape[2]
    specs = lambda keys: [pl.BlockSpec((1, keys, LANES), lambda i, g, j: (i, 0, g)),
                          pl.BlockSpec((1, 1, vrows, keys), lambda i, g, j: (i, g, 0, 0)),
                          pl.BlockSpec(lam_rows.shape, lambda i, g, j: (0, 0)),
                          pl.BlockSpec(subln.shape, lambda i, g, j: (0, 0))]
    return _attention_call(functools.partial(_diff_kernel, lam_init=lam_init), "diff_attention", qbt,
                           (kb, vbt, lam_rows, subln), specs, n_ctx=n_ctx, heads=DIFF_HEADS, bounded=bounded)


def _gdn_prep_kernel(c_ref, cp_ref, cn_ref, g_ref, cw_ref, gp_ref, q_ref, k_ref, v_ref, bg_ref, *, ctx_tiles, nt):
    t = pl.program_id(1)
    x = c_ref[0]
    first = jnp.logical_or(t == 0, t == ctx_tiles)
    last = jnp.logical_or(t == ctx_tiles - 1, t == nt - 1)
    prev_row = jnp.where(first, 0.0, cp_ref[0, 7:8, :])
    next_row = jnp.where(last, 0.0, cn_ref[0, 0:1, :])
    ridx = lax.broadcasted_iota(jnp.int32, x.shape, 0)
    xm = jnp.where(ridx == 0, prev_row, pltpu.roll(x, 1, axis=0))
    xp = jnp.where(ridx == TM - 1, next_row, pltpu.roll(x, TM - 1, axis=0))
    y = _silu(xm * cw_ref[0:1, :] + x * cw_ref[1:2, :] + xp * cw_ref[2:3, :])
    for h in range(GDN_HEADS):
        sl = slice(h * GDN_DK, (h + 1) * GDN_DK)
        qh = y[:, sl]
        q_ref[0, :, sl] = (qh * lax.rsqrt(jnp.sum(qh * qh, axis=-1, keepdims=True) + EPS)
                           * (GDN_DK ** -0.5)).astype(BF16)
        kh = y[:, 512 + h * GDN_DK:512 + (h + 1) * GDN_DK]
        k_ref[0, :, sl] = (kh * lax.rsqrt(jnp.sum(kh * kh, axis=-1, keepdims=True) + EPS)).astype(BF16)
    v_ref[0] = y[:, 1024:1536].astype(BF16)
    gx = g_ref[0]
    lane = lax.broadcasted_iota(jnp.int32, gx.shape, 1)
    a = gx + gp_ref[1:2, :]
    softplus = jnp.maximum(a, 0.0) + jnp.log(1.0 + jnp.exp(-jnp.abs(a)))
    gdec = -jnp.exp(gp_ref[0:1, :]) * softplus
    bg_ref[0] = jnp.where(lane < 8, _sigmoid(gx), jnp.where(lane < 16, gdec, 0.0))


def _gdn_prep(cqkv, gates, conv_w8, gparams, *, ctx_tiles):
    b, t, cch = cqkv.shape
    nt = t // TM
    r8 = TM // 8
    tile = lambda width: pl.BlockSpec((1, TM, width), lambda i, j: (i, j, 0))
    return pl.pallas_call(
        functools.partial(_gdn_prep_kernel, ctx_tiles=ctx_tiles, nt=nt),
        grid=(b, nt),
        in_specs=[tile(cch),
                  pl.BlockSpec((1, 8, cch), lambda i, j: (i, jnp.maximum(j * r8 - 1, 0), 0)),
                  pl.BlockSpec((1, 8, cch), lambda i, j: (i, jnp.minimum((j + 1) * r8, nt * r8 - 1), 0)),
                  tile(LANES),
                  pl.BlockSpec(conv_w8.shape, lambda i, j: (0, 0)),
                  pl.BlockSpec(gparams.shape, lambda i, j: (0, 0))],
        out_specs=(tile(512), tile(512), tile(512), tile(LANES)),
        out_shape=(jax.ShapeDtypeStruct((b, t, 512), BF16), jax.ShapeDtypeStruct((b, t, 512), BF16),
                   jax.ShapeDtypeStruct((b, t, 512), BF16), jax.ShapeDtypeStruct((b, t, LANES), F32)),
        compiler_params=_cparams(("parallel", "parallel")),
        name="gdn_prep",
    )(cqkv, cqkv, cqkv, gates, conv_w8, gparams)


def _unit_tri_inverses(mats):
    n = mats[0].shape[0]
    r = lax.broadcasted_iota(jnp.int32, (n, n), 0)
    c = lax.broadcasted_iota(jnp.int32, (n, n), 1)
    eye = jnp.where(r == c, 1.0, 0.0)
    ps = [-a for a in mats]
    ts = [eye + p for p in ps]
    for _ in range(5):
        ps = [_mm3(p, p) for p in ps]
        ts = [t + _mm3(t, p) for t, p in zip(ts, ps)]
    return ts


def _gdn_chunk_kernel(q_ref, k_ref, v_ref, bg_ref, l1_ref, l2_ref, u_ref, gl_ref):
    L = GDN_CHUNK
    r = lax.broadcasted_iota(jnp.int32, (L, L), 0)
    c = lax.broadcasted_iota(jnp.int32, (L, L), 1)
    chains = [(n, d, h) for n in range(GDN_CHUNKS_PER_STEP) for d in range(2) for h in range(GDN_HEADS)]
    incl = [(r >= c), (r <= c)]
    strict = [(r > c), (r < c)]
    rows = [slice(n * L, (n + 1) * L) for n in range(GDN_CHUNKS_PER_STEP)]
    bgs = [bg_ref[0, rw, :] for rw in rows]
    gcs = [[_mm_exact_lhs(jnp.where(m, 1.0, 0.0).astype(BF16), bg) for m in incl] for bg in bgs]
    gcts = [[g.T for g in gc] for gc in gcs]
    sl = [slice(h * GDN_DK, (h + 1) * GDN_DK) for h in range(GDN_HEADS)]
    qs = [[q_ref[0, rw, s] for s in sl] for rw in rows]
    ks = [[k_ref[0, rw, s] for s in sl] for rw in rows]
    kfs = [[k.astype(F32) for k in kk] for kk in ks]
    gcc = [gcs[n][d][:, 8 + 4 * d + h:9 + 4 * d + h] for n, d, h in chains]
    gcr = [gcts[n][d][8 + 4 * d + h:9 + 4 * d + h, :] for n, d, h in chains]
    dec = [jnp.where(incl[d], jnp.exp(jnp.where(incl[d], gcc[i] - gcr[i], 0.0)), 0.0) for i, (n, d, h) in enumerate(chains)]
    beta = [bgs[n][:, 4 * d + h:4 * d + h + 1] for n, d, h in chains]
    kbeta = [kfs[n][h] * beta[i] for i, (n, d, h) in enumerate(chains)]
    amat = [jnp.where(strict[d], _mm_nt(kbeta[i], ks[n][h]) * dec[i], 0.0) for i, (n, d, h) in enumerate(chains)]
    tinv = _unit_tri_inverses(amat)
    eg = [jnp.exp(g) for g in gcc]
    rhs = [jnp.concatenate([v_ref[0, rows[n], sl[h]].astype(F32) * beta[i], kbeta[i] * eg[i]], axis=1)
           for i, (n, d, h) in enumerate(chains)]
    sol = [_mm3(t, x) for t, x in zip(tinv, rhs)]
    qk = [[_mm_nt(qs[n][h], ks[n][h]) for h in range(GDN_HEADS)] for n in range(GDN_CHUNKS_PER_STEP)]
    for i, (n, d, h) in enumerate(chains):
        last = L - 1 if d == 0 else 0
        glast = gcc[i][last:last + 1, :]
        kdec = kfs[n][h] * jnp.exp(glast - gcc[i])
        u_ref[0, d, n, h] = sol[i][:, :GDN_DK]
        l1_ref[0, d, n, h, 0:L, :] = sol[i][:, GDN_DK:].astype(BF16)
        l1_ref[0, d, n, h, L:2 * L, :] = (qs[n][h].astype(F32) * eg[i]).astype(BF16)
        l2_ref[0, d, n, h, 0:L, :] = (qk[n][h] * dec[i]).astype(BF16)
        l2_ref[0, d, n, h, L:, :] = kdec.T.astype(BF16)
        gl_ref[0, d, n, h:h + 1, :] = jnp.broadcast_to(jnp.exp(glast), (1, LANES))
    for d in range(2):
        for n in range(GDN_CHUNKS_PER_STEP):
            gl_ref[0, d, n, GDN_HEADS:, :] = jnp.zeros((8 - GDN_HEADS, LANES), F32)


GDN_CHUNKS_PER_STEP = 2


def _gdn_chunks(qg, kg, vg, bg):
    b, t, _ = qg.shape
    nc = t // GDN_CHUNK
    L = GDN_CHUNK
    cps = GDN_CHUNKS_PER_STEP
    assert nc % cps == 0
    tile = lambda width: pl.BlockSpec((1, cps * L, width), lambda i, j: (i, j, 0))
    return pl.pallas_call(
        _gdn_chunk_kernel,
        grid=(b, nc // cps),
        in_specs=[tile(512), tile(512), tile(512), tile(LANES)],
        out_specs=(pl.BlockSpec((1, 2, cps, 4, 2 * L, GDN_DK), lambda i, j: (i, 0, j, 0, 0, 0)),
                   pl.BlockSpec((1, 2, cps, 4, L + GDN_DK, L), lambda i, j: (i, 0, j, 0, 0, 0)),
                   pl.BlockSpec((1, 2, cps, 4, L, GDN_DK), lambda i, j: (i, 0, j, 0, 0, 0)),
                   pl.BlockSpec((1, 2, cps, 8, LANES), lambda i, j: (i, 0, j, 0, 0))),
        out_shape=(jax.ShapeDtypeStruct((b, 2, nc, 4, 2 * L, GDN_DK), BF16),
                   jax.ShapeDtypeStruct((b, 2, nc, 4, L + GDN_DK, L), BF16),
                   jax.ShapeDtypeStruct((b, 2, nc, 4, L, GDN_DK), F32),
                   jax.ShapeDtypeStruct((b, 2, nc, 8, LANES), F32)),
        compiler_params=_cparams(("parallel", "parallel")),
        name="gdn_chunks",
    )(qg, kg, vg, bg)


def _gdn_scan_kernel(l1f_ref, l2f_ref, uf_ref, glf_ref, l1b_ref, l2b_ref, ub_ref, glb_ref,
                     of_ref, ob_ref, s_ref):
    L = GDN_CHUNK

    @pl.when(pl.program_id(1) == 0)
    def _():
        s_ref[...] = jnp.zeros_like(s_ref)

    dirs = ((l1f_ref, l2f_ref, uf_ref, glf_ref, of_ref), (l1b_ref, l2b_ref, ub_ref, glb_ref, ob_ref))
    chains = [(d, h) for d in range(2) for h in range(GDN_HEADS)]
    ss = [s_ref[i] for i in range(len(chains))]
    for step in range(GDN_SCAN_CHUNKS):
        ns = (step, GDN_SCAN_CHUNKS - 1 - step)
        r1 = [_mm(dirs[d][0][0, 0, ns[d], h], ss[i]) for i, (d, h) in enumerate(chains)]
        v_new = [dirs[d][2][0, 0, ns[d], h] - r1[i][:L] for i, (d, h) in enumerate(chains)]
        r2 = [_mm(dirs[d][1][0, 0, ns[d], h], v_new[i]) for i, (d, h) in enumerate(chains)]
        for i, (d, h) in enumerate(chains):
            dirs[d][4][0, ns[d] * L:(ns[d] + 1) * L, h * GDN_DK:(h + 1) * GDN_DK] = r1[i][L:] + r2[i][:L]
        ss = [ss[i] * dirs[d][3][0, 0, ns[d], h:h + 1, :] + r2[i][L:] for i, (d, h) in enumerate(chains)]
    for i in range(len(chains)):
        s_ref[i] = ss[i]


GDN_SCAN_CHUNKS = 2


def _gdn_scan(l1, l2, u, gl, *, ctx_chunks):
    b, _, nc = l1.shape[:3]
    L = GDN_CHUNK
    cs = GDN_SCAN_CHUNKS
    assert nc % cs == 0 and ctx_chunks % cs == 0
    nblk, ctx_blk = nc // cs, ctx_chunks // cs

    def rev(j):
        return jnp.where(j < ctx_blk, ctx_blk - 1 - j, nblk - 1 - (j - ctx_blk))

    def spec(a, d):
        blk = (1, 1, cs) + a.shape[3:]
        zeros = (0,) * (a.ndim - 3)
        if d == 0:
            return pl.BlockSpec(blk, lambda i, j: (i, 0, j) + zeros)
        return pl.BlockSpec(blk, lambda i, j: (i, 1, rev(j)) + zeros)

    return pl.pallas_call(
        _gdn_scan_kernel,
        grid=(b, nblk),
        in_specs=[spec(l1, 0), spec(l2, 0), spec(u, 0), spec(gl, 0),
                  spec(l1, 1), spec(l2, 1), spec(u, 1), spec(gl, 1)],
        out_specs=(pl.BlockSpec((1, cs * L, 512), lambda i, j: (i, j, 0)),
                   pl.BlockSpec((1, cs * L, 512), lambda i, j: (i, rev(j), 0))),
        out_shape=(jax.ShapeDtypeStruct((b, nc * L, 512), F32), jax.ShapeDtypeStruct((b, nc * L, 512), F32)),
        scratch_shapes=[pltpu.VMEM((2 * GDN_HEADS, GDN_DK, GDN_DK), F32)],
        compiler_params=_cparams(("parallel", "arbitrary")),
        name="gdn_scan",
    )(l1, l2, u, gl, l1, l2, u, gl)


def _merge_kernel(x_ref, h_ref, yac_ref, yal_ref, ybc_ref, ybl_ref, of_ref, ob_ref, z_ref, mod_ref, gnw_ref, n2w_ref,
                  wg_ref, wb_ref, wo_ref, wr_ref, xo_ref, h2_ref, aff_ref, afft_ref, *, nb, ctx_tiles, d):
    m = _mod_row(mod_ref, nb, ctx_tiles)
    g1, sh2, sc2 = m[:, 2 * d:3 * d], m[:, 3 * d:4 * d], m[:, 4 * d:5 * d]
    o = of_ref[0] + ob_ref[0]
    z = z_ref[0].astype(F32)
    ygs = []
    for h in range(GDN_HEADS):
        sl = slice(h * GDN_DK, (h + 1) * GDN_DK)
        ygs.append(_rms(o[:, sl], gnw_ref[...]) * _silu(z[:, sl]))
    yg = jnp.concatenate(ygs, axis=1).astype(BF16)
    hb = h_ref[0]
    merged = jnp.zeros((TM, d), F32)
    is_ctx = pl.program_id(1) < ctx_tiles
    ya = jnp.where(is_ctx, yac_ref[0], yal_ref[0])
    yb = jnp.where(is_ctx, ybc_ref[0], ybl_ref[0])
    for i, y in enumerate((ya, yb, yg)):
        gate = _sigmoid(jnp.dot(hb, wg_ref[i], preferred_element_type=F32))
        merged = merged + gate * jnp.dot(y, wb_ref[i], preferred_element_type=F32)
    x = x_ref[0] + g1 * jnp.dot(merged.astype(BF16), wo_ref[...], preferred_element_type=F32)
    xo_ref[0] = x
    h2 = _rms(x, n2w_ref[...]) * (1.0 + sc2) + sh2
    h2b = h2.astype(BF16)
    h2_ref[0] = h2b
    logits = _mm3(h2, wr_ref[...])
    lane = lax.broadcasted_iota(jnp.int32, logits.shape, 1)
    logits = jnp.where(lane < N_EXPERTS, logits, -jnp.inf)
    e = jnp.exp(logits - jnp.max(logits, axis=-1, keepdims=True))
    aff = e / jnp.sum(e, axis=-1, keepdims=True)
    aff_ref[0] = aff
    afft_ref[0] = aff.T[:N_EXPERTS]


def _merge(xs, h, ya, yb, o_f, o_b, z, mod, gnw, n2w, wg, wb, wo, wr, *, nb, ctx_tiles):
    b, t, d = xs.shape
    nt = t // TM
    tile = lambda width: pl.BlockSpec((1, TM, width), lambda i, j: (i, j, 0))
    whole = lambda a: pl.BlockSpec(a.shape, lambda i, j: (0,) * a.ndim)
    (yac, yal), (ybc, ybl) = ya, yb
    ctx_tile = pl.BlockSpec((1, TM, 512), lambda i, j: (i, jnp.minimum(j, ctx_tiles - 1), 0))
    lat_tile = pl.BlockSpec((1, TM, 512), lambda i, j: (i, jnp.maximum(j - ctx_tiles, 0), 0))
    return pl.pallas_call(
        functools.partial(_merge_kernel, nb=nb, ctx_tiles=ctx_tiles, d=d),
        grid=(b, nt),
        in_specs=[tile(d), tile(d), ctx_tile, lat_tile, ctx_tile, lat_tile, tile(512), tile(512), tile(512),
                  whole(mod), whole(gnw), whole(n2w), whole(wg), whole(wb), whole(wo), whole(wr)],
        out_specs=(tile(d), tile(d), tile(LANES), pl.BlockSpec((1, N_EXPERTS, TM), lambda i, j: (i, 0, j))),
        out_shape=(jax.ShapeDtypeStruct((b, t, d), F32), jax.ShapeDtypeStruct((b, t, d), BF16),
                   jax.ShapeDtypeStruct((b, t, LANES), F32), jax.ShapeDtypeStruct((b, N_EXPERTS, t), F32)),
        compiler_params=_cparams(("parallel", "parallel")),
        name="merge",
    )(xs, h, yac, yal, ybc, ybl, o_f, o_b, z, mod, gnw, n2w, wg, wb, wo, wr)


ROW_ALIGN = 16
SLOT_GROUP = 64
FFN_CHUNKS = 6


def _moe_rows(caps, nt):
    chunk = -(-(sum(caps) + (ROW_ALIGN - 1) * nt) // (ROW_ALIGN * FFN_CHUNKS)) * ROW_ALIGN
    return chunk, FFN_CHUNKS * chunk + TM


def _route_kernel(afft_ref, post_ref, posc_ref, starts_ref, *, ctx_tiles, nt, cap_ctx, cap_lat):
    a = afft_ref[0]
    bits = pltpu.bitcast(a, jnp.int32)
    lane = lax.broadcasted_iota(jnp.int32, a.shape, 1)
    isctx = lane < ctx_tiles * TM

    def bisect(i, carry):
        tc, tl = carry
        bit = jnp.left_shift(jnp.int32(1), 30 - i)
        cc, cl = tc | bit, tl | bit
        ge = bits >= jnp.where(isctx, cc, cl)
        n_c = jnp.sum(jnp.where(jnp.logical_and(ge, isctx), 1.0, 0.0), axis=-1, keepdims=True)
        n_l = jnp.sum(jnp.where(jnp.logical_and(ge, jnp.logical_not(isctx)), 1.0, 0.0), axis=-1, keepdims=True)
        return jnp.where(n_c >= cap_ctx, cc, tc), jnp.where(n_l >= cap_lat, cl, tl)

    zero = jnp.zeros((N_EXPERTS, 1), jnp.int32)
    tc, tl = lax.fori_loop(0, 31, bisect, (zero, zero))
    thr = jnp.where(isctx, tc, tl)
    gt = bits > thr
    eq = bits == thr
    n_gt_c = jnp.sum(jnp.where(jnp.logical_and(gt, isctx), 1.0, 0.0), axis=-1, keepdims=True)
    n_gt_l = jnp.sum(jnp.where(jnp.logical_and(gt, jnp.logical_not(isctx)), 1.0, 0.0), axis=-1, keepdims=True)
    r = lax.broadcasted_iota(jnp.int32, (TM, TM), 0)
    c = lax.broadcasted_iota(jnp.int32, (TM, TM), 1)
    upper = jnp.where(r < c, 1.0, 0.0).astype(BF16)
    lane_t = lax.broadcasted_iota(jnp.int32, (N_EXPERTS, LANES), 1)
    starts = jnp.zeros((N_EXPERTS, LANES), F32)
    start = jnp.zeros((N_EXPERTS, 1), F32)
    eq_seen = jnp.zeros((N_EXPERTS, 1), F32)
    for j in range(nt):
        sl = slice(j * TM, (j + 1) * TM)
        if j == ctx_tiles:
            eq_seen = jnp.zeros((N_EXPERTS, 1), F32)
        need = (cap_ctx - n_gt_c) if j < ctx_tiles else (cap_lat - n_gt_l)
        eq_j = jnp.where(eq[:, sl], 1.0, 0.0)
        rank = eq_seen + _mm(eq_j, upper)
        eq_seen = eq_seen + jnp.sum(eq_j, axis=-1, keepdims=True)
        sel = jnp.logical_or(gt[:, sl], jnp.logical_and(eq[:, sl], rank < need))
        sel_f = jnp.where(sel, 1.0, 0.0)
        pos = jnp.where(sel, start + _mm(sel_f, upper), -1.0)
        post_ref[0, :, sl] = pos
        posc_ref[0, sl, :] = jnp.concatenate([pos, jnp.full((LANES - N_EXPERTS, TM), -1.0, F32)], axis=0).T
        starts = jnp.where(lane_t == j, start, starts)
        cnt = jnp.sum(sel_f, axis=-1, keepdims=True)
        start = start + jnp.floor((cnt + (ROW_ALIGN - 1)) * (1.0 / ROW_ALIGN)) * ROW_ALIGN
    starts_ref[0] = jnp.where(lane_t == nt, start, starts).astype(jnp.int32)


def _route(afft, *, ctx_tiles, cap_ctx, cap_lat):
    b, e, t = afft.shape
    nt = t // TM
    return pl.pallas_call(
        functools.partial(_route_kernel, ctx_tiles=ctx_tiles, nt=nt, cap_ctx=cap_ctx, cap_lat=cap_lat),
        grid=(b,),
        in_specs=[pl.BlockSpec((1, e, t), lambda i: (i, 0, 0))],
        out_specs=(pl.BlockSpec((1, e, t), lambda i: (i, 0, 0)),
                   pl.BlockSpec((1, t, LANES), lambda i: (i, 0, 0)),
                   pl.BlockSpec((1, e, LANES), lambda i: (i, 0, 0))),
        out_shape=(jax.ShapeDtypeStruct((b, e, t), F32), jax.ShapeDtypeStruct((b, t, LANES), F32),
                   jax.ShapeDtypeStruct((b, e, LANES), jnp.int32)),
        compiler_params=_cparams(("parallel",)),
        name="moe_route",
    )(afft)


def _moe_ffn_kernel(starts_ref, h2_ref, post_ref, wg_ref, wu_ref, wd_ref, y_ref, xin_ref, *, nt, unroll, chunk, n_picks):
    b, e = pl.program_id(0), pl.program_id(1)
    xin_ref[n_picks:, :] = jnp.zeros((xin_ref.shape[0] - n_picks, xin_ref.shape[1]), F32)
    slot = lax.broadcasted_iota(jnp.int32, (SLOT_GROUP, TM), 0).astype(F32)

    def picks(j, g):
        base = (b * N_EXPERTS + e) * LANES + j
        off = starts_ref[base]
        tok = pl.multiple_of(j * TM, TM)
        pos = post_ref[0, pl.ds(e, 1), pl.ds(tok, TM)] - (off + g * SLOT_GROUP).astype(F32)
        onehot = jnp.where(pos == slot, 1.0, 0.0).astype(BF16)
        blk = jnp.dot(onehot, h2_ref[0, pl.ds(tok, TM), :], preferred_element_type=F32)
        return pl.multiple_of(off + g * SLOT_GROUP, ROW_ALIGN), starts_ref[base + 1] - off - g * SLOT_GROUP, blk

    def first_groups(i, carry):
        blocks = [picks(i * unroll + k, 0) for k in range(unroll)]
        for row0, _, blk in blocks:
            xin_ref[pl.ds(row0, SLOT_GROUP), :] = blk
        return carry

    lax.fori_loop(0, nt // unroll, first_groups, 0)

    def more_groups(j, carry):
        def group(g, carry):
            row0, n_real, blk = picks(j, g)
            keep = lax.broadcasted_iota(jnp.int32, blk.shape, 0) < n_real
            xin_ref[pl.ds(row0, SLOT_GROUP), :] = jnp.where(keep, blk, xin_ref[pl.ds(row0, SLOT_GROUP), :])
            return carry

        base = (b * N_EXPERTS + e) * LANES + j
        n_groups = (starts_ref[base + 1] - starts_ref[base] + (SLOT_GROUP - 1)) // SLOT_GROUP
        return lax.fori_loop(1, n_groups, group, carry)

    lax.fori_loop(0, nt, more_groups, 0)

    def ffn(c, carry):
        rows = pl.ds(pl.multiple_of(c * chunk, ROW_ALIGN), chunk)
        x = xin_ref[rows, :].astype(BF16)
        hid = _silu(jnp.dot(x, wg_ref[0], preferred_element_type=F32)) * jnp.dot(x, wu_ref[0], preferred_element_type=F32)
        y_ref[0, 0, rows, :] = jnp.dot(hid.astype(BF16), wd_ref[0], preferred_element_type=F32).astype(BF16)
        return carry

    n_used = (starts_ref[(b * N_EXPERTS + e) * LANES + nt] + chunk - 1) // chunk
    lax.fori_loop(0, n_used, ffn, 0)

    def blank(c, carry):
        y_ref[0, 0, pl.ds(pl.multiple_of(c * chunk, ROW_ALIGN), chunk), :] = jnp.zeros((chunk, y_ref.shape[3]), BF16)
        return carry

    lax.fori_loop(n_used, FFN_CHUNKS, blank, 0)
    y_ref[0, 0, FFN_CHUNKS * chunk:, :] = jnp.zeros((TM, y_ref.shape[3]), BF16)


def _moe_ffn(starts, h2, post, wg, wu, wd, *, chunk, rows, n_picks):
    b, t, d = h2.shape
    nt = t // TM
    e, _, ff = wg.shape
    grid_spec = pltpu.PrefetchScalarGridSpec(
        num_scalar_prefetch=1,
        grid=(b, e),
        in_specs=[pl.BlockSpec((1, t, d), lambda i, k, s: (i, 0, 0), pipeline_mode=pl.Buffered(1)),
                  pl.BlockSpec((1, e, t), lambda i, k, s: (i, 0, 0)),
                  pl.BlockSpec((1, d, ff), lambda i, k, s: (k, 0, 0)),
                  pl.BlockSpec((1, d, ff), lambda i, k, s: (k, 0, 0)),
                  pl.BlockSpec((1, ff, d), lambda i, k, s: (k, 0, 0))],
        out_specs=pl.BlockSpec((1, 1, rows, d), lambda i, k, s: (i, k, 0, 0)),
        scratch_shapes=[pltpu.VMEM((rows, d), F32)],
    )
    return pl.pallas_call(
        functools.partial(_moe_ffn_kernel, nt=nt, unroll=max(u for u in range(1, 12) if nt % u == 0), chunk=chunk,
                          n_picks=n_picks),
        grid_spec=grid_spec,
        out_shape=jax.ShapeDtypeStruct((b, e, rows, d), BF16),
        compiler_params=pltpu.CompilerParams(dimension_semantics=("parallel", "arbitrary"),
                                             vmem_limit_bytes=MOE_FFN_VMEM_LIMIT),
        name="moe_ffn",
    )(starts, h2, post, wg, wu, wd)


def _moe_combine_kernel(starts_ref, y_ref, posc_ref, aff_ref, x_ref, g2_ref, o_ref, *, nb, ctx_tiles):
    b, j = pl.program_id(0), pl.program_id(2)
    slot = lax.broadcasted_iota(jnp.int32, (TM, TM), 1).astype(F32)
    posc, aff = posc_ref[0], aff_ref[0]
    acc = jnp.zeros(o_ref.shape[1:], F32)
    for e in range(N_EXPERTS):
        off = pl.multiple_of(starts_ref[(b * N_EXPERTS + e) * LANES + j], ROW_ALIGN)
        pos = posc[:, e:e + 1]
        onehot = jnp.where(pos - off.astype(F32) == slot, 1.0, 0.0).astype(BF16)
        w = jnp.where(pos >= 0.0, aff[:, e:e + 1], 0.0)
        acc = acc + w * jnp.dot(onehot, y_ref[0, e, pl.ds(off, TM), :], preferred_element_type=F32)
    row = jnp.where(j < ctx_tiles, nb, b)
    o_ref[0] = x_ref[0] + g2_ref[pl.ds(row, 1), :] * acc


def _moe_combine(starts, y, posc, aff, xs, g2, *, nb, ctx_tiles, dsplit=2):
    b, t, d = xs.shape
    e, rows = y.shape[1:3]
    dq = d // dsplit
    grid_spec = pltpu.PrefetchScalarGridSpec(
        num_scalar_prefetch=1,
        grid=(b, dsplit, t // TM),
        in_specs=[pl.BlockSpec((1, e, rows, dq), lambda i, q, j, s: (i, 0, 0, q), pipeline_mode=pl.Buffered(1)),
                  pl.BlockSpec((1, TM, LANES), lambda i, q, j, s: (i, j, 0)),
                  pl.BlockSpec((1, TM, LANES), lambda i, q, j, s: (i, j, 0)),
                  pl.BlockSpec((1, TM, dq), lambda i, q, j, s: (i, j, q)),
                  pl.BlockSpec((MOD_ROWS, dq), lambda i, q, j, s: (0, q))],
        out_specs=pl.BlockSpec((1, TM, dq), lambda i, q, j, s: (i, j, q)),
    )
    return pl.pallas_call(
        functools.partial(_moe_combine_kernel, nb=nb, ctx_tiles=ctx_tiles),
        grid_spec=grid_spec,
        out_shape=jax.ShapeDtypeStruct((b, t, d), F32),
        compiler_params=_cparams(("parallel", "parallel", "parallel")),
        name="moe_combine",
    )(starts, y, posc, aff, xs, g2)


def _lambda_init(layer_idx):
    return 0.8 - 0.6 * math.exp(-0.3 * layer_idx)


def _rope_tables(n_lat, n_ctx):
    rows = n_lat // GRID_W
    row = jnp.repeat(jnp.arange(rows, dtype=F32), GRID_W)
    col = jnp.tile(jnp.arange(GRID_W, dtype=F32), rows)
    axis_dim = HEAD_DIM // 2
    inv_freq = ROPE_THETA ** (-jnp.arange(0, axis_dim, 2, dtype=F32) / axis_dim)
    ang_r, ang_c = row[:, None] * inv_freq, col[:, None] * inv_freq
    cos = jnp.concatenate([jnp.cos(ang_r)] * 2 + [jnp.cos(ang_c)] * 2, axis=-1)
    sin = jnp.concatenate([-jnp.sin(ang_r), jnp.sin(ang_r), -jnp.sin(ang_c), jnp.sin(ang_c)], axis=-1)
    cos = jnp.concatenate([jnp.ones((n_ctx, HEAD_DIM), F32), cos], axis=0)
    sin = jnp.concatenate([jnp.zeros((n_ctx, HEAD_DIM), F32), sin], axis=0)
    return jnp.tile(cos, (1, 2)), jnp.tile(sin, (1, 2))


def kernel(x, c, ctx, c_ctx, w_mod, b_mod, norm1_w, norm2_w, w_in, gqa_q_norm, gqa_k_norm, diff_q_norm,
           diff_k_norm, diff_lambda_q1, diff_lambda_k1, diff_lambda_q2, diff_lambda_k2, diff_subln,
           gdn_conv_w, gdn_a_log, gdn_dt_bias, gdn_norm_w, w_merge_gate, w_branch, w_out, w_router,
           w_exp_gate, w_exp_up, w_exp_down):
    nb, n_lat, d = x.shape
    n_ctx = ctx.shape[1]
    depth = w_mod.shape[0]
    assert n_ctx % TM == 0 and n_lat % TM == 0 and nb < MOD_ROWS and d == 1024
    ctx_tiles = n_ctx // TM
    ctx_chunks = n_ctx // GDN_CHUNK

    cos_t, sin_t = _rope_tables(n_lat, n_ctx)
    cc = jnp.zeros((MOD_ROWS, d), F32).at[:nb].set(c).at[nb].set(c_ctx)
    mods = _modulation(cc, w_mod, b_mod)
    half = lax.broadcasted_iota(jnp.int32, (LANES, LANES), 0) // HEAD_DIM
    bd = (half == half.T).astype(BF16)

    xs = jnp.concatenate([ctx, x], axis=1)
    scale = HEAD_DIM ** -0.5 * math.log2(math.e)
    for li in range(depth):
        lam_init = _lambda_init(li)
        w_in_p = jnp.pad(w_in[li], ((0, 0), (0, 4480 - w_in.shape[-1]))).astype(BF16)
        qkn = jnp.stack([jnp.tile(gqa_q_norm[li] * scale, 2), jnp.tile(gqa_k_norm[li], 2),
                         jnp.tile(diff_q_norm[li] * scale, 2), jnp.tile(diff_k_norm[li], 2)])
        (h, qa, ka, va, qb, kb, vb, cqkv, z, gates) = _inproj(
            xs, mods[li], norm1_w[li][None, :], w_in_p, cos_t, sin_t, qkn, bd, nb=nb, ctx_tiles=ctx_tiles)

        def score_bound(qn, kn):
            return HEAD_DIM * jnp.max(jnp.abs(qn * scale)) * jnp.max(jnp.abs(kn))

        ya = lax.cond(score_bound(gqa_q_norm[li], gqa_k_norm[li]) <= SCORE_LIMIT,
                      functools.partial(_gqa_attention, n_ctx=n_ctx, bounded=True),
                      functools.partial(_gqa_attention, n_ctx=n_ctx, bounded=False), qa, ka, va)
        lam_rows = jnp.stack([jnp.concatenate([diff_lambda_q1[li], diff_lambda_q2[li]]),
                              jnp.concatenate([diff_lambda_k1[li], diff_lambda_k2[li]])])
        yb = lax.cond(score_bound(diff_q_norm[li], diff_k_norm[li]) <= SCORE_LIMIT,
                      functools.partial(_diff_attention, n_ctx=n_ctx, lam_init=lam_init, bounded=True),
                      functools.partial(_diff_attention, n_ctx=n_ctx, lam_init=lam_init, bounded=False),
                      qb, kb, vb, lam_rows, diff_subln[li][None, :])

        conv_w8 = jnp.pad(gdn_conv_w[li], ((0, 5), (0, 0)))
        gparams = jnp.zeros((2, LANES), F32)
        gparams = gparams.at[0, 8:16].set(gdn_a_log[li].reshape(-1)).at[1, 8:16].set(gdn_dt_bias[li].reshape(-1))
        qg, kg, vg, bg = _gdn_prep(cqkv, gates, conv_w8, gparams, ctx_tiles=ctx_tiles)
        l1, l2, u, gl = _gdn_chunks(qg, kg, vg, bg)
        o_f, o_b = _gdn_scan(l1, l2, u, gl, ctx_chunks=ctx_chunks)

        wr = jnp.pad(w_router[li], ((0, 0), (0, LANES - N_EXPERTS)))
        xs, h2, aff, afft = _merge(xs, h, ya, yb, o_f, o_b, z, mods[li], gdn_norm_w[li][None, :],
                                   norm2_w[li][None, :], w_merge_gate[li].astype(BF16), w_branch[li].astype(BF16),
                                   w_out[li].astype(BF16), wr, nb=nb, ctx_tiles=ctx_tiles)

        caps = (CAPACITY_FACTOR * n_ctx // N_EXPERTS, CAPACITY_FACTOR * n_lat // N_EXPERTS)
        chunk, rows = _moe_rows(caps, (n_ctx + n_lat) // TM)
        post, posc, starts = _route(afft, ctx_tiles=ctx_tiles, cap_ctx=caps[0], cap_lat=caps[1])
        starts = starts.reshape(-1)
        y = _moe_ffn(starts, h2, post, w_exp_gate[li].astype(BF16), w_exp_up[li].astype(BF16),
                     w_exp_down[li].astype(BF16), chunk=chunk, rows=rows, n_picks=sum(caps))
        xs = _moe_combine(starts, y, posc, aff, xs, mods[li, :, 5 * d:], nb=nb, ctx_tiles=ctx_tiles)
    return xs[:, n_ctx:]
```
